```python
import math
import jax, jax.numpy as jnp
from jax import lax
import numpy as np

D_MODEL = 4096
BATCH = 1
SEQ = 8192
DEPTH = 1
DEC_BATCH = 8
DEC_SEQ = 2048
PAST_LEN = 128

N_HEADS = 32
N_KV_HEADS = 8
HEAD_DIM = D_MODEL // N_HEADS
ATTN_WIDTH = N_HEADS * HEAD_DIM
KV_WIDTH = N_KV_HEADS * HEAD_DIM
WINDOW = 128
BLOCK = 128
SSM_HEAD_DIM = 64
SSM_HEADS = D_MODEL // SSM_HEAD_DIM
SSM_INNER = SSM_HEADS * SSM_HEAD_DIM
SSM_GROUPS = 8
SSM_STATE = 128
XBC_WIDTH = SSM_INNER + 2 * SSM_GROUPS * SSM_STATE
CONV_WIDTH = 5
CHUNK = 128
DT_MIN = 1e-3
DT_MAX = 1e-1
DT_PROJ_SCALE = 0.1
D_FF = 11008
ALPHA = (2 * DEPTH) ** 0.25
BETA = (8 * DEPTH) ** -0.25
LN_EPS = 1e-5
RMS_EPS = 1e-5
IN_SIZES = (ATTN_WIDTH, KV_WIDTH, KV_WIDTH, SSM_INNER, XBC_WIDTH, 2 * SSM_HEADS, 2 * D_MODEL)
IN_WIDTH = ATTN_WIDTH + 2 * KV_WIDTH + SSM_INNER + XBC_WIDTH + 2 * SSM_HEADS + 2 * D_MODEL
DT_COL_START = ATTN_WIDTH + 2 * KV_WIDTH + SSM_INNER + XBC_WIDTH

kernel_name = "hybrid_bidir_swa_ssd_macaron_deepnorm"


def layer_norm(x, g, b):
    xf = x.astype(jnp.float32)
    mu = jnp.mean(xf, axis=-1, keepdims=True)
    xc = xf - mu
    var = jnp.mean(xc * xc, axis=-1, keepdims=True)
    y = xc * lax.rsqrt(var + LN_EPS) * g.astype(jnp.float32) + b.astype(jnp.float32)
    return y.astype(x.dtype)


def swiglu(x, w_gate, w_up, w_down):
    return (jax.nn.silu(x @ w_gate) * (x @ w_up)) @ w_down


def windowed_attention(q, k, v, sink):
    b, s = q.shape[0], q.shape[1]
    nb = s // BLOCK
    r = N_HEADS // N_KV_HEADS
    qb = q.reshape(b, nb, BLOCK, N_KV_HEADS, r, HEAD_DIM).transpose(1, 0, 2, 3, 4, 5)
    pad = ((0, 0), (BLOCK, BLOCK), (0, 0), (0, 0))
    kp = jnp.pad(k, pad).reshape(b, nb + 2, BLOCK, N_KV_HEADS, HEAD_DIM)
    vp = jnp.pad(v, pad).reshape(b, nb + 2, BLOCK, N_KV_HEADS, HEAD_DIM)
    kw = jnp.concatenate([kp[:, :-2], kp[:, 1:-1], kp[:, 2:]], axis=2).transpose(1, 0, 2, 3, 4)
    vw = jnp.concatenate([vp[:, :-2], vp[:, 1:-1], vp[:, 2:]], axis=2).transpose(1, 0, 2, 3, 4)
    slopes = (2.0 ** (-8.0 * jnp.arange(1, N_HEADS + 1, dtype=jnp.float32) / N_HEADS)).reshape(N_KV_HEADS, r)
    sink_f = sink.astype(jnp.float32).reshape(N_KV_HEADS, r)[None, :, :, None, None]
    rel = jnp.arange(BLOCK)[:, None] - jnp.arange(3 * BLOCK)[None, :] + BLOCK
    dist = jnp.abs(rel)
    alibi = -slopes[:, :, None, None] * dist.astype(jnp.float32)
    scale = HEAD_DIM ** -0.5

    def block_fn(args):
        qi, ki, vi, i = args
        key_pos = i * BLOCK - BLOCK + jnp.arange(3 * BLOCK)
        valid = (dist <= WINDOW) & ((key_pos >= 0) & (key_pos < s))[None, :]
        sc = jnp.einsum("bqgrd,bkgd->bgrqk", qi, ki, preferred_element_type=jnp.float32) * scale + alibi
        sc = jnp.where(valid, sc, -1e30)
        m = jnp.maximum(jnp.max(sc, axis=-1, keepdims=True), sink_f)
        p = jnp.exp(sc - m)
        denom = jnp.sum(p, axis=-1, keepdims=True) + jnp.exp(sink_f - m)
        return jnp.einsum("bgrqk,bkgd->bqgrd", (p / denom).astype(vi.dtype), vi)

    out = lax.map(block_fn, (qb, kw, vw, jnp.arange(nb)))
    return out.transpose(1, 0, 2, 3, 4, 5).reshape(b, s, N_HEADS * HEAD_DIM)


def centred_dwconv(u, w, bias):
    half = CONV_WIDTH // 2
    out = lax.conv_general_dilated(u, w[:, None, :], window_strides=(1,), padding=[(half, half)],
                                   dimension_numbers=("NWC", "WIO", "NWC"),
                                   feature_group_count=u.shape[-1])
    return out + bias


def ssd_scan(x, dt, a, bm, cm):
    b, s = x.shape[0], x.shape[1]
    nc = s // CHUNK
    r = SSM_HEADS // SSM_GROUPS
    xr = (x * dt[..., None]).reshape(b, nc, CHUNK, SSM_GROUPS, r, SSM_HEAD_DIM)
    da = (dt * a).reshape(b, nc, CHUNK, SSM_GROUPS, r).transpose(0, 3, 4, 1, 2)
    bc = bm.reshape(b, nc, CHUNK, SSM_GROUPS, SSM_STATE)
    cc = cm.reshape(b, nc, CHUNK, SSM_GROUPS, SSM_STATE)
    cum = jnp.cumsum(da, axis=-1)
    seg = cum[..., :, None] - cum[..., None, :]
    causal = jnp.tril(jnp.ones((CHUNK, CHUNK), dtype=bool))
    lmat = jnp.exp(jnp.where(causal, seg, -jnp.inf))
    cb = jnp.einsum("bclgn,bcsgn->bgcls", cc, bc)
    y_diag = jnp.einsum("bgcls,bgrcls,bcsgrp->bclgrp", cb, lmat, xr)
    decay_to_end = jnp.exp(cum[..., -1:] - cum)
    chunk_states = jnp.einsum("bclgn,bgrcl,bclgrp->cbgrpn", bc, decay_to_end, xr)
    chunk_decay = jnp.exp(cum[..., -1]).transpose(3, 0, 1, 2)

    def step(h, inp):
        st, dec = inp
        return h * dec[..., None, None] + st, h

    h0 = jnp.zeros((b, SSM_GROUPS, r, SSM_HEAD_DIM, SSM_STATE), jnp.float32)
    _, prev = lax.scan(step, h0, (chunk_states, chunk_decay))
    y_off = jnp.einsum("bclgn,cbgrpn,bgrcl->bclgrp", cc, prev, jnp.exp(cum))
    return (y_diag + y_off).reshape(b, s, SSM_HEADS, SSM_HEAD_DIM)


def token_mixer(x, w_in, conv_w, conv_b, dt_bias, a_log, d_skip, ssm_norm_w, sink, gate_bias, w_out):
    b, s = x.shape[0], x.shape[1]
    f32 = jnp.float32
    proj = x @ w_in
    splits = np.cumsum(np.array(IN_SIZES))[:-1].tolist()
    q, k, v, z, xbc, dt_raw, gate_logits = jnp.split(proj, splits, axis=-1)
    attn = windowed_attention(q.reshape(b, s, N_HEADS, HEAD_DIM),
                              k.reshape(b, s, N_KV_HEADS, HEAD_DIM),
                              v.reshape(b, s, N_KV_HEADS, HEAD_DIM), sink)
    xbc = jax.nn.silu(centred_dwconv(xbc, conv_w, conv_b))
    xs, bm, cm = jnp.split(xbc, [SSM_INNER, SSM_INNER + SSM_GROUPS * SSM_STATE], axis=-1)
    xs = xs.reshape(b, s, SSM_HEADS, SSM_HEAD_DIM).astype(f32)
    bm = bm.reshape(b, s, SSM_GROUPS, SSM_STATE).astype(f32)
    cm = cm.reshape(b, s, SSM_GROUPS, SSM_STATE).astype(f32)
    dt = jax.nn.softplus(dt_raw.astype(f32).reshape(b, s, 2, SSM_HEADS) + dt_bias.astype(f32))
    a = -jnp.exp(a_log.astype(f32))
    flip = lambda t: jnp.flip(t, axis=1)
    y_fwd = ssd_scan(xs, dt[:, :, 0], a[0], bm, cm)
    y_bwd = flip(ssd_scan(flip(xs), flip(dt[:, :, 1]), a[1], flip(bm), flip(cm)))
    y = y_fwd + y_bwd + d_skip.astype(f32)[:, None] * xs
    y = y.reshape(b, s, SSM_INNER) * jax.nn.silu(z.astype(f32))
    yg = y.reshape(b, s, SSM_GROUPS, SSM_INNER // SSM_GROUPS)
    yg = yg * lax.rsqrt(jnp.mean(yg * yg, axis=-1, keepdims=True) + RMS_EPS)
    ssm = (yg.reshape(b, s, SSM_INNER) * ssm_norm_w.astype(f32)).astype(x.dtype)
    g = jax.nn.sigmoid(gate_logits + gate_bias)
    h = g[..., :D_MODEL] * attn + g[..., D_MODEL:] * ssm
    return h @ w_out


def encoder_layer(x, ffn1_w_gate, ffn1_w_up, ffn1_w_down, ln1_g, ln1_b,
                  w_in, conv_w, conv_b, dt_bias, a_log, d_skip, ssm_norm_w, sink, gate_bias, w_out,
                  ln2_g, ln2_b, ffn2_w_gate, ffn2_w_up, ffn2_w_down, ln3_g, ln3_b):
    x = layer_norm(ALPHA * x + 0.5 * swiglu(x, ffn1_w_gate, ffn1_w_up, ffn1_w_down), ln1_g, ln1_b)
    x = layer_norm(ALPHA * x + token_mixer(x, w_in, conv_w, conv_b, dt_bias, a_log, d_skip,
                                           ssm_norm_w, sink, gate_bias, w_out), ln2_g, ln2_b)
    x = layer_norm(ALPHA * x + 0.5 * swiglu(x, ffn2_w_gate, ffn2_w_up, ffn2_w_down), ln3_g, ln3_b)
    return x


def run_trunk(x, params):
    for layer in range(DEPTH):
        x = encoder_layer(x, *[p[layer] for p in params])
    return x


def setup_inputs(seed: int = 0) -> dict:
    key = jax.random.key(seed)
    ks = jax.random.split(key, 26)
    f32 = jnp.float32
    L = DEPTH

    def nrm(k, shape, scale):
        return jax.random.normal(k, shape, f32) * scale

    col_scale = jnp.ones((IN_WIDTH,), f32).at[DT_COL_START:DT_COL_START + 2 * SSM_HEADS].set(DT_PROJ_SCALE)
    dt0 = jnp.exp(jax.random.uniform(ks[10], (L, 2, SSM_HEADS), f32, math.log(DT_MIN), math.log(DT_MAX)))
    return {
        "x_prompt": nrm(ks[0], (BATCH, SEQ, D_MODEL), 1.0),
        "x_sample": nrm(ks[1], (DEC_BATCH, DEC_SEQ, D_MODEL), 1.0),
        "ffn1_w_gate": nrm(ks[2], (L, D_MODEL, D_FF), D_MODEL ** -0.5),
        "ffn1_w_up": nrm(ks[3], (L, D_MODEL, D_FF), D_MODEL ** -0.5),
        "ffn1_w_down": nrm(ks[4], (L, D_FF, D_MODEL), BETA * D_FF ** -0.5),
        "ln1_g": 1.0 + nrm(ks[5], (L, D_MODEL), 0.01),
        "ln1_b": nrm(ks[6], (L, D_MODEL), 0.01),
        "w_in": nrm(ks[7], (L, D_MODEL, IN_WIDTH), D_MODEL ** -0.5) * col_scale,
        "conv_w": nrm(ks[8], (L, CONV_WIDTH, XBC_WIDTH), CONV_WIDTH ** -0.5),
        "conv_b": nrm(ks[9], (L, XBC_WIDTH), 0.01),
        "dt_bias": dt0 + jnp.log(-jnp.expm1(-dt0)),
        "a_log": jnp.log(jax.random.uniform(ks[11], (L, 2, SSM_HEADS), f32, 1.0, 16.0)),
        "d_skip": 1.0 + nrm(ks[12], (L, SSM_HEADS), 0.01),
        "ssm_norm_w": 1.0 + nrm(ks[13], (L, SSM_INNER), 0.01),
        "sink": nrm(ks[14], (L, N_HEADS), 0.5),
        "gate_bias": nrm(ks[15], (L, 2 * D_MODEL), 0.01),
        "w_out": nrm(ks[16], (L, D_MODEL, D_MODEL), BETA * D_MODEL ** -0.5),
        "ln2_g": 1.0 + nrm(ks[17], (L, D_MODEL), 0.01),
        "ln2_b": nrm(ks[18], (L, D_MODEL), 0.01),
        "ffn2_w_gate": nrm(ks[19], (L, D_MODEL, D_FF), D_MODEL ** -0.5),
        "ffn2_w_up": nrm(ks[20], (L, D_MODEL, D_FF), D_MODEL ** -0.5),
        "ffn2_w_down": nrm(ks[21], (L, D_FF, D_MODEL), BETA * D_FF ** -0.5),
        "ln3_g": 1.0 + nrm(ks[22], (L, D_MODEL), 0.01),
        "ln3_b": nrm(ks[23], (L, D_MODEL), 0.01),
    }


def reference(x_prompt, x_sample, ffn1_w_gate, ffn1_w_up, ffn1_w_down, ln1_g, ln1_b,
              w_in, conv_w, conv_b, dt_bias, a_log, d_skip, ssm_norm_w, sink, gate_bias, w_out,
              ln2_g, ln2_b, ffn2_w_gate, ffn2_w_up, ffn2_w_down, ln3_g, ln3_b):
    params = (ffn1_w_gate, ffn1_w_up, ffn1_w_down, ln1_g, ln1_b,
              w_in, conv_w, conv_b, dt_bias, a_log, d_skip, ssm_norm_w, sink, gate_bias, w_out,
              ln2_g, ln2_b, ffn2_w_gate, ffn2_w_up, ffn2_w_down, ln3_g, ln3_b)
    y_prompt = run_trunk(x_prompt, params)
    y_sample = run_trunk(x_sample, params)
    return (y_prompt, y_sample)
```

```python
import functools

import jax
import jax.numpy as jnp
from jax import lax
from jax.experimental import pallas as pl
from jax.experimental.pallas import tpu as pltpu

F32 = jnp.float32
BF16 = jnp.bfloat16

D_MODEL = 4096
DEPTH = 1
N_HEADS = 32
N_KV_HEADS = 8
HEAD_DIM = 128
KV_REP = N_HEADS // N_KV_HEADS
ATTN_WIDTH = N_HEADS * HEAD_DIM
KV_WIDTH = N_KV_HEADS * HEAD_DIM
WINDOW = 128
BLOCK = 128
SSM_HEAD_DIM = 64
SSM_HEADS = 64
SSM_INNER = 4096
SSM_GROUPS = 8
SSM_STATE = 128
HEADS_PER_GROUP = SSM_HEADS // SSM_GROUPS
GROUP_WIDTH = HEADS_PER_GROUP * SSM_HEAD_DIM
XBC_WIDTH = SSM_INNER + 2 * SSM_GROUPS * SSM_STATE
CONV_WIDTH = 5
CONV_HALF = CONV_WIDTH // 2
CHUNK = 128
D_FF = 11008
ALPHA = (2 * DEPTH) ** 0.25
LN_EPS = 1e-5
RMS_EPS = 1e-5

LANES = 128
SUBLANES = 8
VMEM_LIMIT = 56 * 1024 * 1024

REST_Z = 0
REST_XBC = SSM_INNER
REST_GATE = REST_XBC + XBC_WIDTH
REST_DT = REST_GATE + 2 * D_MODEL
REST_TILE = 512
REST_WIDTH = ((REST_DT + 2 * SSM_HEADS + REST_TILE - 1) // REST_TILE) * REST_TILE
QKV_WIDTH = ATTN_WIDTH + 2 * KV_WIDTH


def _params(semantics):
    return pltpu.CompilerParams(dimension_semantics=semantics, vmem_limit_bytes=VMEM_LIMIT)


def _layer_norm_rows(y, g, b):
    mu = jnp.mean(y, axis=-1, keepdims=True)
    yc = y - mu
    var = jnp.mean(yc * yc, axis=-1, keepdims=True)
    return yc * lax.rsqrt(var + LN_EPS) * g + b


def _residual_ln_inplace(x_ref, o_ref, g_ref, b_ref, branch_scale, ob_ref=None, rows=64):
    g = g_ref[...]
    b = b_ref[...]

    def body(r, carry):
        sl = pl.ds(pl.multiple_of(r * rows, rows), rows)
        y = ALPHA * x_ref[sl, :] + branch_scale * o_ref[sl, :]
        out = _layer_norm_rows(y, g, b)
        o_ref[sl, :] = out
        if ob_ref is not None:
            ob_ref[sl, :] = out.astype(BF16)
        return carry

    lax.fori_loop(0, o_ref.shape[0] // rows, body, 0)


def _ffn_ln_kernel(x_ref, wg_ref, wu_ref, wd_ref, g_ref, b_ref, *rest, emit_bf16):
    if emit_bf16:
        o_ref, ob_ref, xb_ref = rest
    else:
        (o_ref, xb_ref), ob_ref = rest, None
    j = pl.program_id(1)

    @pl.when(j == 0)
    def _():
        xb_ref[...] = x_ref[...].astype(BF16)
        o_ref[...] = jnp.zeros_like(o_ref)

    xb = xb_ref[...]
    gate = jnp.dot(xb, wg_ref[...], preferred_element_type=F32)
    up = jnp.dot(xb, wu_ref[...], preferred_element_type=F32)
    h = (gate * jax.nn.sigmoid(gate) * up).astype(BF16)
    o_ref[...] += jnp.dot(h, wd_ref[...], preferred_element_type=F32)

    @pl.when(j == pl.num_programs(1) - 1)
    def _():
        _residual_ln_inplace(x_ref, o_ref, g_ref, b_ref, 0.5, ob_ref)


def _ffn_ln(x, wg, wu, wd, ln_g, ln_b, emit_bf16, tm=512, tf=256):
    m, d = x.shape
    tm = min(tm, m)
    dff = wg.shape[1]
    grid = (m // tm, dff // tf)
    out_shape = [jax.ShapeDtypeStruct((m, d), F32)]
    out_specs = [pl.BlockSpec((tm, d), lambda i, j: (i, 0))]
    if emit_bf16:
        out_shape.append(jax.ShapeDtypeStruct((m, d), BF16))
        out_specs.append(pl.BlockSpec((tm, d), lambda i, j: (i, 0)))
    res = pl.pallas_call(
        functools.partial(_ffn_ln_kernel, emit_bf16=emit_bf16),
        grid=grid,
        in_specs=[
            pl.BlockSpec((tm, d), lambda i, j: (i, 0), pipeline_mode=pl.Buffered(1)),
            pl.BlockSpec((d, tf), lambda i, j: (0, j)),
            pl.BlockSpec((d, tf), lambda i, j: (0, j)),
            pl.BlockSpec((tf, d), lambda i, j: (j, 0)),
            pl.BlockSpec((1, d), lambda i, j: (0, 0)),
            pl.BlockSpec((1, d), lambda i, j: (0, 0)),
        ],
        out_specs=out_specs,
        out_shape=out_shape,
        scratch_shapes=[pltpu.VMEM((tm, d), BF16)],
        compiler_params=_params(("parallel", "arbitrary")),
        name="ffn_ln",
    )(x, wg, wu, wd, ln_g, ln_b)
    return res if emit_bf16 else res[0]


def _matmul_kernel(x_ref, w_ref, o_ref):
    o_ref[...] = jnp.dot(x_ref[...], w_ref[...], preferred_element_type=F32).astype(o_ref.dtype)


def _matmul(x, w, out_dtype, tm=1024, tn=512, name="matmul"):
    m, k = x.shape
    tm = min(tm, m)
    n = w.shape[1]
    return pl.pallas_call(
        _matmul_kernel,
        grid=(m // tm, n // tn),
        in_specs=[
            pl.BlockSpec((tm, k), lambda i, j: (i, 0)),
            pl.BlockSpec((k, tn), lambda i, j: (0, j)),
        ],
        out_specs=pl.BlockSpec((tm, tn), lambda i, j: (i, j)),
        out_shape=jax.ShapeDtypeStruct((m, n), out_dtype),
        compiler_params=_params(("parallel", "arbitrary")),
        name=name,
    )(x, w)


def _attn_kernel(hp_ref, q_ref, kp_ref, kc_ref, kn_ref, vp_ref, vc_ref, vn_ref, o_ref):
    i = pl.program_id(1)
    g = pl.program_id(2)
    nb = pl.num_programs(1)
    q = q_ref[...]
    qs = jnp.concatenate([q[:, r * HEAD_DIM:(r + 1) * HEAD_DIM] for r in range(KV_REP)], axis=0)
    k = jnp.concatenate([kp_ref[...], kc_ref[...], kn_ref[...]], axis=0)
    v = jnp.concatenate([vp_ref[...], vc_ref[...], vn_ref[...]], axis=0)
    s = lax.dot_general(qs, k, (((1,), (1,)), ((), ())), preferred_element_type=F32)
    s = s * (HEAD_DIM ** -0.5)

    t_idx = lax.broadcasted_iota(jnp.int32, (BLOCK, 3 * BLOCK), 0)
    s_idx = lax.broadcasted_iota(jnp.int32, (BLOCK, 3 * BLOCK), 1)
    dist = jnp.abs(t_idx - s_idx + BLOCK)
    in_seq = ((s_idx >= BLOCK) | (i > 0)) & ((s_idx < 2 * BLOCK) | (i < nb - 1))
    valid = (dist <= WINDOW) & in_seq
    dist_f = dist.astype(F32)

    probs = []
    for r in range(KV_REP):
        head = g * KV_REP + r
        slope = hp_ref[0, head]
        sink = hp_ref[1, head]
        sc = s[r * BLOCK:(r + 1) * BLOCK] - slope * dist_f
        sc = jnp.where(valid, sc, -1e30)
        m = jnp.maximum(jnp.max(sc, axis=-1, keepdims=True), sink)
        p = jnp.exp(sc - m)
        denom = jnp.sum(p, axis=-1, keepdims=True) + jnp.exp(sink - m)
        probs.append((p * (1.0 / denom)).astype(BF16))
    pn = jnp.concatenate(probs, axis=0)
    out = jnp.dot(pn, v, preferred_element_type=F32)
    for r in range(KV_REP):
        o_ref[:, r * HEAD_DIM:(r + 1) * HEAD_DIM] = out[r * BLOCK:(r + 1) * BLOCK]


def _attention(qkv, head_params, batch, seq):
    nb = seq // BLOCK
    qkv3 = qkv.reshape(batch, seq, QKV_WIDTH)
    k_col = ATTN_WIDTH // HEAD_DIM
    v_col = (ATTN_WIDTH + KV_WIDTH) // HEAD_DIM
    qw = KV_REP * HEAD_DIM

    def band(col0, shift):
        def index(b, i, g):
            return (b, jnp.clip(i + shift, 0, nb - 1), col0 + g)
        return pl.BlockSpec((None, BLOCK, HEAD_DIM), index)

    out = pl.pallas_call(
        _attn_kernel,
        grid=(batch, nb, N_KV_HEADS),
        in_specs=[
            pl.BlockSpec(memory_space=pltpu.SMEM),
            pl.BlockSpec((None, BLOCK, qw), lambda b, i, g: (b, i, g)),
            band(k_col, -1), band(k_col, 0), band(k_col, 1),
            band(v_col, -1), band(v_col, 0), band(v_col, 1),
        ],
        out_specs=pl.BlockSpec((None, BLOCK, qw), lambda b, i, g: (b, i, g)),
        out_shape=jax.ShapeDtypeStruct((batch, seq, ATTN_WIDTH), F32),
        compiler_params=_params(("parallel", "parallel", "parallel")),
        name="attention",
    )(head_params, qkv3, qkv3, qkv3, qkv3, qkv3, qkv3, qkv3)
    return out.reshape(batch * seq, ATTN_WIDTH)


def _conv_kernel(prev_ref, cur_ref, next_ref, w_ref, b_ref, o_ref, ext_ref):
    i = pl.program_id(1)
    tb = cur_ref.shape[0]
    zeros = jnp.zeros(prev_ref.shape, F32)
    ext_ref[0:SUBLANES, :] = jnp.where(i > 0, prev_ref[...], zeros)
    ext_ref[SUBLANES:SUBLANES + tb, :] = cur_ref[...]
    ext_ref[SUBLANES + tb:2 * SUBLANES + tb, :] = jnp.where(i < pl.num_programs(1) - 1, next_ref[...], zeros)
    acc = jnp.zeros(cur_ref.shape, F32)
    for k in range(CONV_WIDTH):
        acc = acc + w_ref[k:k + 1, :] * ext_ref[pl.ds(SUBLANES - CONV_HALF + k, tb), :]
    acc = acc + b_ref[...]
    o_ref[...] = acc * jax.nn.sigmoid(acc)


def _conv_silu(rest, conv_w, conv_b, batch, seq, tb=512, tc=512):
    tb = min(tb, seq)
    rest3 = rest.reshape(batch, seq, REST_WIDTH)
    nt = seq // tb
    col0 = REST_XBC // tc
    sub_per_blk = tb // SUBLANES
    last_sub = seq // SUBLANES - 1
    out = pl.pallas_call(
        _conv_kernel,
        grid=(batch, nt, XBC_WIDTH // tc),
        in_specs=[
            pl.BlockSpec((None, SUBLANES, tc),
                         lambda b, i, c: (b, jnp.maximum(i * sub_per_blk - 1, 0), col0 + c)),
            pl.BlockSpec((None, tb, tc), lambda b, i, c: (b, i, col0 + c)),
            pl.BlockSpec((None, SUBLANES, tc),
                         lambda b, i, c: (b, jnp.minimum((i + 1) * sub_per_blk, last_sub), col0 + c)),
            pl.BlockSpec((CONV_WIDTH, tc), lambda b, i, c: (0, c)),
            pl.BlockSpec((1, tc), lambda b, i, c: (0, c)),
        ],
        out_specs=pl.BlockSpec((None, tb, tc), lambda b, i, c: (b, i, c)),
        out_shape=jax.ShapeDtypeStruct((batch, seq, XBC_WIDTH), F32),
        scratch_shapes=[pltpu.VMEM((tb + 2 * SUBLANES, tc), F32)],
        compiler_params=_params(("parallel", "parallel", "parallel")),
        name="conv_silu",
    )(rest3, rest3, rest3, conv_w, conv_b)
    return out.reshape(batch * seq, XBC_WIDTH)


def _split3(x):
    hi = x.astype(BF16)
    r1 = x - hi.astype(F32)
    mid = r1.astype(BF16)
    lo = (r1 - mid.astype(F32)).astype(BF16)
    return hi, mid, lo


def _dt_kernel(raw_ref, bias_ref, alog_ref, dt_ref, cum_ref, cumt_ref):
    x = raw_ref[...] + bias_ref[...]
    dt = jnp.maximum(x, 0.0) + jnp.log1p(jnp.exp(-jnp.abs(x)))
    da = dt * (-jnp.exp(alog_ref[...]))
    t_idx = lax.broadcasted_iota(jnp.int32, (CHUNK, CHUNK), 0)
    s_idx = lax.broadcasted_iota(jnp.int32, (CHUNK, CHUNK), 1)
    lower = (s_idx <= t_idx).astype(BF16)
    upper = (s_idx >= t_idx).astype(BF16)
    cum_f = jnp.zeros(da.shape, F32)
    cum_r = jnp.zeros(da.shape, F32)
    for piece in _split3(da):
        cum_f = cum_f + jnp.dot(lower, piece, preferred_element_type=F32)
        cum_r = cum_r + jnp.dot(upper, piece, preferred_element_type=F32)
    col = lax.broadcasted_iota(jnp.int32, da.shape, 1)
    cum = jnp.where(col < SSM_HEADS, cum_f, cum_r)
    dt_ref[...] = dt
    cum_ref[...] = cum
    cumt_ref[...] = cum.T


def _dt_prep(rest, dt_bias, a_log):
    m = rest.shape[0]
    nchunks = m // CHUNK
    w = 2 * SSM_HEADS
    dt, cum, cumt = pl.pallas_call(
        _dt_kernel,
        grid=(nchunks,),
        in_specs=[
            pl.BlockSpec((CHUNK, w), lambda c: (c, REST_DT // w)),
            pl.BlockSpec((1, w), lambda c: (0, 0)),
            pl.BlockSpec((1, w), lambda c: (0, 0)),
        ],
        out_specs=[
            pl.BlockSpec((CHUNK, w), lambda c: (c, 0)),
            pl.BlockSpec((CHUNK, w), lambda c: (c, 0)),
            pl.BlockSpec((w, CHUNK), lambda c: (c, 0)),
        ],
        out_shape=[
            jax.ShapeDtypeStruct((m, w), F32),
            jax.ShapeDtypeStruct((m, w), F32),
            jax.ShapeDtypeStruct((nchunks * w, CHUNK), F32),
        ],
        compiler_params=_params(("parallel",)),
        name="dt_prep",
    )(rest, dt_bias, a_log)
    hg = HEADS_PER_GROUP
    dt_g = dt.reshape(m, 2, SSM_GROUPS, hg).transpose(1, 2, 0, 3)
    cum_g = cum.reshape(m, 2, SSM_GROUPS, hg).transpose(1, 2, 0, 3)
    cumt_g = cumt.reshape(nchunks, 2, SSM_GROUPS, hg, CHUNK).transpose(1, 2, 0, 3, 4)
    return dt_g, cum_g, cumt_g


def _ssd_kernel(xs_ref, b_ref, c_ref, dt_ref, cum_ref, cumt_ref, o_ref, state_ref):
    fwd = pl.program_id(1) == 0
    hg = HEADS_PER_GROUP
    p = SSM_HEAD_DIM

    @pl.when(pl.program_id(3) == 0)
    def _():
        state_ref[...] = jnp.zeros_like(state_ref)

    xs = xs_ref[...]
    bm = b_ref[...].astype(BF16)
    cm = c_ref[...].astype(BF16)
    dt = dt_ref[...]
    cum = cum_ref[...]
    cumt = cumt_ref[...]
    tot = jnp.where(fwd, cum[CHUNK - 1:CHUNK, :], cum[0:1, :])
    small = jnp.concatenate(
        [dt, dt * jnp.exp(tot - cum), jnp.exp(cum), jnp.broadcast_to(jnp.exp(tot), (SUBLANES, hg))], axis=0)
    head_of_lane = lax.broadcasted_iota(jnp.int32, (hg, GROUP_WIDTH), 1) // p
    expand = (head_of_lane == lax.broadcasted_iota(jnp.int32, (hg, GROUP_WIDTH), 0)).astype(F32)
    wide = jnp.dot(small, expand, preferred_element_type=F32, precision=lax.Precision.HIGHEST)
    dt_w = wide[0:CHUNK]
    todecay_w = wide[CHUNK:2 * CHUNK]
    ecum_w = wide[2 * CHUNK:3 * CHUNK]
    etot_w = wide[3 * CHUNK:3 * CHUNK + 1]
    xdt = (xs * dt_w).astype(BF16)
    xend = (xs * todecay_w).astype(BF16)

    cb = lax.dot_general(cm, bm, (((1,), (1,)), ((), ())), preferred_element_type=F32)
    state = state_ref[...]
    y_off = jnp.dot(cm, state.astype(BF16), preferred_element_type=F32) * ecum_w

    t_idx = lax.broadcasted_iota(jnp.int32, (CHUNK, CHUNK), 0)
    s_idx = lax.broadcasted_iota(jnp.int32, (CHUNK, CHUNK), 1)
    mask = (t_idx - s_idx) * jnp.where(fwd, 1, -1) >= 0
    lane = lax.broadcasted_iota(jnp.int32, (CHUNK, 2 * p), 1)
    for pair in range(hg // 2):
        mats = []
        for h in (2 * pair, 2 * pair + 1):
            seg = cum[:, h:h + 1] - cumt[h:h + 1, :]
            decay = jnp.exp(jnp.where(mask, seg, -jnp.inf))
            mats.append((cb * decay).astype(BF16))
        lhs = jnp.concatenate(mats, axis=1)
        xp = xdt[:, pair * 2 * p:(pair + 1) * 2 * p]
        zero = jnp.zeros_like(xp)
        rhs = jnp.concatenate([jnp.where(lane < p, xp, zero), jnp.where(lane >= p, xp, zero)], axis=0)
        y_diag = jnp.dot(lhs, rhs, preferred_element_type=F32)
        sl = slice(pair * 2 * p, (pair + 1) * 2 * p)
        o_ref[:, sl] = y_off[:, sl] + y_diag

    new_state = lax.dot_general(bm, xend, (((0,), (0,)), ((), ())), preferred_element_type=F32)
    state_ref[...] = state * etot_w + new_state


def _ssd(xa, dt_g, cum_g, cumt_g, batch, seq):
    m = batch * seq
    nc = seq // CHUNK
    b_col = SSM_INNER // SSM_STATE
    c_col = b_col + SSM_GROUPS

    def row(b, d, c):
        return b * nc + c + d * (nc - 1 - 2 * c)

    return pl.pallas_call(
        _ssd_kernel,
        grid=(batch, 2, SSM_GROUPS, nc),
        in_specs=[
            pl.BlockSpec((CHUNK, GROUP_WIDTH), lambda b, d, g, c: (row(b, d, c), g)),
            pl.BlockSpec((CHUNK, SSM_STATE), lambda b, d, g, c: (row(b, d, c), b_col + g)),
            pl.BlockSpec((CHUNK, SSM_STATE), lambda b, d, g, c: (row(b, d, c), c_col + g)),
            pl.BlockSpec((None, None, CHUNK, HEADS_PER_GROUP), lambda b, d, g, c: (d, g, row(b, d, c), 0)),
            pl.BlockSpec((None, None, CHUNK, HEADS_PER_GROUP), lambda b, d, g, c: (d, g, row(b, d, c), 0)),
            pl.BlockSpec((None, None, None, HEADS_PER_GROUP, CHUNK),
                         lambda b, d, g, c: (d, g, row(b, d, c), 0, 0)),
        ],
        out_specs=pl.BlockSpec((None, CHUNK, GROUP_WIDTH), lambda b, d, g, c: (d, row(b, d, c), g)),
        out_shape=jax.ShapeDtypeStruct((2, m, SSM_INNER), F32),
        scratch_shapes=[pltpu.VMEM((SSM_STATE, GROUP_WIDTH), F32)],
        compiler_params=_params(("parallel", "parallel", "parallel", "arbitrary")),
        name="ssd_scan",
    )(xa, xa, xa, dt_g, cum_g, cumt_g)


def _merge_kernel(attn_ref, yf_ref, yb_ref, xs_ref, z_ref, ga_ref, gb_ref, dsk_ref, nw_ref, ba_ref, bb_ref, o_ref):
    z = z_ref[...]
    y = (yf_ref[...] + yb_ref[...] + dsk_ref[...] * xs_ref[...]) * (z * jax.nn.sigmoid(z))
    ms = jnp.mean(y * y, axis=-1, keepdims=True)
    ssm = y * lax.rsqrt(ms + RMS_EPS) * nw_ref[...]
    gate_a = jax.nn.sigmoid(ga_ref[...] + ba_ref[...])
    gate_b = jax.nn.sigmoid(gb_ref[...] + bb_ref[...])
    o_ref[...] = (gate_a * attn_ref[...] + gate_b * ssm).astype(o_ref.dtype)


def _merge(attn, y2, xa, rest, d_skip_wide, norm_w, gate_bias, tm=512):
    m = attn.shape[0]
    tm = min(tm, m)
    gw = GROUP_WIDTH
    z0 = REST_Z // gw
    ga0 = REST_GATE // gw
    gb0 = (REST_GATE + D_MODEL) // gw
    nb = D_MODEL // gw

    def blk(col0):
        return pl.BlockSpec((tm, gw), lambda i, g: (i, col0 + g))

    def vec(col0):
        return pl.BlockSpec((1, gw), lambda i, g: (0, col0 + g))

    return pl.pallas_call(
        _merge_kernel,
        grid=(m // tm, SSM_GROUPS),
        in_specs=[
            blk(0),
            pl.BlockSpec((None, tm, gw), lambda i, g: (0, i, g)),
            pl.BlockSpec((None, tm, gw), lambda i, g: (1, i, g)),
            blk(0), blk(z0), blk(ga0), blk(gb0),
            vec(0), vec(0), vec(0), vec(nb),
        ],
        out_specs=blk(0),
        out_shape=jax.ShapeDtypeStruct((m, D_MODEL), BF16),
        compiler_params=_params(("parallel", "parallel")),
        name="merge",
    )(attn, y2, y2, xa, rest, rest, rest, d_skip_wide, norm_w, gate_bias, gate_bias)


def _proj_ln_kernel(h_ref, w_ref, x_ref, g_ref, b_ref, o_ref):
    j = pl.program_id(1)

    @pl.when(j == 0)
    def _():
        o_ref[...] = jnp.zeros_like(o_ref)

    o_ref[...] += jnp.dot(h_ref[...], w_ref[...], preferred_element_type=F32)

    @pl.when(j == pl.num_programs(1) - 1)
    def _():
        _residual_ln_inplace(x_ref, o_ref, g_ref, b_ref, 1.0)


def _proj_ln(h, w, x, ln_g, ln_b, tm=512, tk=512):
    m, k = h.shape
    tm = min(tm, m)
    d = w.shape[1]
    return pl.pallas_call(
        _proj_ln_kernel,
        grid=(m // tm, k // tk),
        in_specs=[
            pl.BlockSpec((tm, tk), lambda i, j: (i, j)),
            pl.BlockSpec((tk, d), lambda i, j: (j, 0)),
            pl.BlockSpec((tm, d), lambda i, j: (i, 0)),
            pl.BlockSpec((1, d), lambda i, j: (0, 0)),
            pl.BlockSpec((1, d), lambda i, j: (0, 0)),
        ],
        out_specs=pl.BlockSpec((tm, d), lambda i, j: (i, 0)),
        out_shape=jax.ShapeDtypeStruct((m, d), F32),
        compiler_params=_params(("parallel", "arbitrary")),
        name="proj_ln",
    )(h, w, x, ln_g, ln_b)


def _prepare(ffn1_w_gate, ffn1_w_up, ffn1_w_down, ln1_g, ln1_b, w_in, conv_w, conv_b, dt_bias, a_log,
             d_skip, ssm_norm_w, sink, gate_bias, w_out, ln2_g, ln2_b,
             ffn2_w_gate, ffn2_w_up, ffn2_w_down, ln3_g, ln3_b):
    row = lambda v: v.reshape(1, -1).astype(F32)
    dt0 = QKV_WIDTH + SSM_INNER + XBC_WIDTH
    dt1 = dt0 + 2 * SSM_HEADS
    w_rest = jnp.concatenate(
        [w_in[:, QKV_WIDTH:dt0], w_in[:, dt1:], w_in[:, dt0:dt1],
         jnp.zeros((D_MODEL, REST_WIDTH - (REST_DT + 2 * SSM_HEADS)), w_in.dtype)], axis=1)
    slopes = 2.0 ** (-8.0 * jnp.arange(1, N_HEADS + 1, dtype=F32) / N_HEADS)
    return dict(
        ffn1=(ffn1_w_gate.astype(BF16), ffn1_w_up.astype(BF16), ffn1_w_down.astype(BF16), row(ln1_g), row(ln1_b)),
        w_qkv=w_in[:, :QKV_WIDTH].astype(BF16),
        w_rest=w_rest.astype(BF16),
        conv_w=conv_w.astype(F32), conv_b=row(conv_b),
        dt_bias=row(dt_bias), a_log=row(a_log),
        d_skip_wide=row(jnp.repeat(d_skip.astype(F32), SSM_HEAD_DIM)),
        norm_w=row(ssm_norm_w),
        head_params=jnp.stack([slopes, sink.astype(F32)]),
        gate_bias=row(gate_bias),
        w_out=w_out.astype(BF16), ln2=(row(ln2_g), row(ln2_b)),
        ffn2=(ffn2_w_gate.astype(BF16), ffn2_w_up.astype(BF16), ffn2_w_down.astype(BF16), row(ln3_g), row(ln3_b)),
    )


def _layer(x, p):
    batch, seq, d = x.shape
    x2d = x.reshape(batch * seq, d)
    x1, x1b = _ffn_ln(x2d, *p["ffn1"], emit_bf16=True)
    qkv = _matmul(x1b, p["w_qkv"], BF16, name="proj_qkv")
    rest = _matmul(x1b, p["w_rest"], F32, name="proj_rest")
    attn = _attention(qkv, p["head_params"], batch, seq)
    xa = _conv_silu(rest, p["conv_w"], p["conv_b"], batch, seq)
    dt_g, cum_g, cumt_g = _dt_prep(rest, p["dt_bias"], p["a_log"])
    y2 = _ssd(xa, dt_g, cum_g, cumt_g, batch, seq)
    h = _merge(attn, y2, xa, rest, p["d_skip_wide"], p["norm_w"], p["gate_bias"])
    x2 = _proj_ln(h, p["w_out"], x1, *p["ln2"])
    y = _ffn_ln(x2, *p["ffn2"], emit_bf16=False)
    return y.reshape(batch, seq, d)


def kernel(x_prompt, x_sample, ffn1_w_gate, ffn1_w_up, ffn1_w_down, ln1_g, ln1_b, w_in, conv_w, conv_b, dt_bias, a_log, d_skip, ssm_norm_w, sink, gate_bias, w_out, ln2_g, ln2_b, ffn2_w_gate, ffn2_w_up, ffn2_w_down, ln3_g, ln3_b):
    weights = (ffn1_w_gate, ffn1_w_up, ffn1_w_down, ln1_g, ln1_b, w_in, conv_w, conv_b, dt_bias, a_log,
               d_skip, ssm_norm_w, sink, gate_bias, w_out, ln2_g, ln2_b,
               ffn2_w_gate, ffn2_w_up, ffn2_w_down, ln3_g, ln3_b)
    p = _prepare(*[w[0] for w in weights])
    return (_layer(x_prompt, p), _layer(x_sample, p))
```

```python
import functools

import jax
import jax.numpy as jnp
from jax import lax
from jax.experimental import pallas as pl
from jax.experimental.pallas import tpu as pltpu

F32 = jnp.float32
BF16 = jnp.bfloat16

D_MODEL = 4096
DEPTH = 1
N_HEADS = 32
N_KV_HEADS = 8
HEAD_DIM = 128
KV_REP = N_HEADS // N_KV_HEADS
ATTN_WIDTH = N_HEADS * HEAD_DIM
KV_WIDTH = N_KV_HEADS * HEAD_DIM
WINDOW = 128
BLOCK = 128
SSM_HEAD_DIM = 64
SSM_HEADS = 64
SSM_INNER = 4096
SSM_GROUPS = 8
SSM_STATE = 128
HEADS_PER_GROUP = SSM_HEADS // SSM_GROUPS
GROUP_WIDTH = HEADS_PER_GROUP * SSM_HEAD_DIM
XBC_WIDTH = SSM_INNER + 2 * SSM_GROUPS * SSM_STATE
CONV_WIDTH = 5
CONV_HALF = CONV_WIDTH // 2
CHUNK = 128
D_FF = 11008
ALPHA = (2 * DEPTH) ** 0.25
LN_EPS = 1e-5
RMS_EPS = 1e-5

SUBLANES = 8
BF16_SUBLANES = 16
VMEM_LIMIT = 56 * 1024 * 1024

REST_Z = 0
REST_GATE = SSM_INNER
REST_XBC = REST_GATE + 2 * D_MODEL
REST_DT = REST_XBC + XBC_WIDTH
REST_TILE = 512
REST_WIDTH = ((REST_DT + 2 * SSM_HEADS + REST_TILE - 1) // REST_TILE) * REST_TILE
QKV_WIDTH = ATTN_WIDTH + 2 * KV_WIDTH


def _params(semantics):
    return pltpu.CompilerParams(dimension_semantics=semantics, vmem_limit_bytes=VMEM_LIMIT)


def _layer_norm_rows(y, g, b):
    mu = jnp.mean(y, axis=-1, keepdims=True)
    yc = y - mu
    var = jnp.mean(yc * yc, axis=-1, keepdims=True)
    return yc * lax.rsqrt(var + LN_EPS) * g + b


def _residual_ln_inplace(x_ref, o_ref, g_ref, b_ref, branch_scale, ob_ref=None, rows=64):
    g = g_ref[...]
    b = b_ref[...]

    def body(r, carry):
        sl = pl.ds(pl.multiple_of(r * rows, rows), rows)
        y = ALPHA * x_ref[sl, :] + branch_scale * o_ref[sl, :]
        out = _layer_norm_rows(y, g, b)
        o_ref[sl, :] = out
        if ob_ref is not None:
            ob_ref[sl, :] = out.astype(BF16)
        return carry

    lax.fori_loop(0, o_ref.shape[0] // rows, body, 0)


def _ffn_ln_kernel(x_ref, wg_ref, wu_ref, wd_ref, g_ref, b_ref, *rest, emit_bf16):
    if emit_bf16:
        o_ref, ob_ref, xb_ref = rest
    else:
        (o_ref, xb_ref), ob_ref = rest, None
    j = pl.program_id(1)

    @pl.when(j == 0)
    def _():
        xb_ref[...] = x_ref[...].astype(BF16)
        o_ref[...] = jnp.zeros_like(o_ref)

    xb = xb_ref[...]
    gate = jnp.dot(xb, wg_ref[...], preferred_element_type=F32)
    up = jnp.dot(xb, wu_ref[...], preferred_element_type=F32)
    h = (gate * jax.nn.sigmoid(gate) * up).astype(BF16)
    o_ref[...] += jnp.dot(h, wd_ref[...], preferred_element_type=F32)

    @pl.when(j == pl.num_programs(1) - 1)
    def _():
        _residual_ln_inplace(x_ref, o_ref, g_ref, b_ref, 0.5, ob_ref)


def _ffn_ln(x, wg, wu, wd, ln_g, ln_b, emit_bf16, tm=512, tf=256):
    m, d = x.shape
    tm = min(tm, m)
    dff = wg.shape[1]
    out_shape = [jax.ShapeDtypeStruct((m, d), F32)]
    out_specs = [pl.BlockSpec((tm, d), lambda i, j: (i, 0))]
    if emit_bf16:
        out_shape.append(jax.ShapeDtypeStruct((m, d), BF16))
        out_specs.append(pl.BlockSpec((tm, d), lambda i, j: (i, 0)))
    res = pl.pallas_call(
        functools.partial(_ffn_ln_kernel, emit_bf16=emit_bf16),
        grid=(m // tm, dff // tf),
        in_specs=[
            pl.BlockSpec((tm, d), lambda i, j: (i, 0), pipeline_mode=pl.Buffered(1)),
            pl.BlockSpec((d, tf), lambda i, j: (0, j)),
            pl.BlockSpec((d, tf), lambda i, j: (0, j)),
            pl.BlockSpec((tf, d), lambda i, j: (j, 0)),
            pl.BlockSpec((1, d), lambda i, j: (0, 0)),
            pl.BlockSpec((1, d), lambda i, j: (0, 0)),
        ],
        out_specs=out_specs,
        out_shape=out_shape,
        scratch_shapes=[pltpu.VMEM((tm, d), BF16)],
        compiler_params=_params(("parallel", "arbitrary")),
        name="ffn_ln",
    )(x, wg, wu, wd, ln_g, ln_b)
    return res if emit_bf16 else res[0]


def _matmul_kernel(x_ref, w_ref, o_ref):
    o_ref[...] = jnp.dot(x_ref[...], w_ref[...], preferred_element_type=F32).astype(o_ref.dtype)


def _matmul(x, w, out_dtype, tm=1024, tn=512, name="matmul"):
    m, k = x.shape
    tm = min(tm, m)
    n = w.shape[1]
    return pl.pallas_call(
        _matmul_kernel,
        grid=(m // tm, n // tn),
        in_specs=[
            pl.BlockSpec((tm, k), lambda i, j: (i, 0)),
            pl.BlockSpec((k, tn), lambda i, j: (0, j)),
        ],
        out_specs=pl.BlockSpec((tm, tn), lambda i, j: (i, j)),
        out_shape=jax.ShapeDtypeStruct((m, n), out_dtype),
        compiler_params=_params(("parallel", "arbitrary")),
        name=name,
    )(x, w)


def _attn_kernel(hp_ref, q_ref, kp_ref, kc_ref, kn_ref, vp_ref, vc_ref, vn_ref, ga_ref, ba_ref, o_ref):
    i = pl.program_id(1)
    nb = pl.num_programs(1)
    kv_heads = kc_ref.shape[1] // HEAD_DIM

    t_idx = lax.broadcasted_iota(jnp.int32, (BLOCK, 3 * BLOCK), 0)
    s_idx = lax.broadcasted_iota(jnp.int32, (BLOCK, 3 * BLOCK), 1)
    dist = jnp.abs(t_idx - s_idx + BLOCK)
    in_seq = ((s_idx >= BLOCK) | (i > 0)) & ((s_idx < 2 * BLOCK) | (i < nb - 1))
    valid = (dist <= WINDOW) & in_seq
    dist_f = dist.astype(F32)

    for kv in range(kv_heads):
        kcols = slice(kv * HEAD_DIM, (kv + 1) * HEAD_DIM)
        q0 = kv * KV_REP * HEAD_DIM
        qs = jnp.concatenate([q_ref[:, q0 + r * HEAD_DIM:q0 + (r + 1) * HEAD_DIM] for r in range(KV_REP)], axis=0)
        k = jnp.concatenate([kp_ref[:, kcols], kc_ref[:, kcols], kn_ref[:, kcols]], axis=0)
        v = jnp.concatenate([vp_ref[:, kcols], vc_ref[:, kcols], vn_ref[:, kcols]], axis=0)
        s = lax.dot_general(qs, k, (((1,), (1,)), ((), ())), preferred_element_type=F32)
        s = s * (HEAD_DIM ** -0.5)
        probs = []
        for r in range(KV_REP):
            head = (pl.program_id(2) * kv_heads + kv) * KV_REP + r
            slope = hp_ref[0, head]
            sink = hp_ref[1, head]
            sc = s[r * BLOCK:(r + 1) * BLOCK] - slope * dist_f
            sc = jnp.where(valid, sc, -1e30)
            m = jnp.maximum(jnp.max(sc, axis=-1, keepdims=True), sink)
            p = jnp.exp(sc - m)
            denom = jnp.sum(p, axis=-1, keepdims=True) + jnp.exp(sink - m)
            probs.append((p * (1.0 / denom)).astype(BF16))
        pn = jnp.concatenate(probs, axis=0)
        out = jnp.dot(pn, v, preferred_element_type=F32)
        for r in range(KV_REP):
            cols = slice(q0 + r * HEAD_DIM, q0 + (r + 1) * HEAD_DIM)
            gate = jax.nn.sigmoid(ga_ref[:, cols] + ba_ref[:, cols])
            o_ref[:, cols] = gate * out[r * BLOCK:(r + 1) * BLOCK]


ATTN_KV_PER_STEP = 8


def _attention(qkv, rest, head_params, gate_bias, batch, seq):
    nb = seq // BLOCK
    qkv3 = qkv.reshape(batch, seq, QKV_WIDTH)
    rest3 = rest.reshape(batch, seq, REST_WIDTH)
    kvw = ATTN_KV_PER_STEP * HEAD_DIM
    k_col = ATTN_WIDTH // kvw
    v_col = (ATTN_WIDTH + KV_WIDTH) // kvw
    qw = ATTN_KV_PER_STEP * KV_REP * HEAD_DIM
    assert REST_GATE % qw == 0 and ATTN_WIDTH % kvw == 0 and KV_WIDTH % kvw == 0
    ga_col = REST_GATE // qw

    def band(col0, shift):
        def index(b, i, g):
            return (b, jnp.clip(i + shift, 0, nb - 1), col0 + g)
        return pl.BlockSpec((None, BLOCK, kvw), index)

    out = pl.pallas_call(
        _attn_kernel,
        grid=(batch, nb, N_KV_HEADS // ATTN_KV_PER_STEP),
        in_specs=[
            pl.BlockSpec(memory_space=pltpu.SMEM),
            pl.BlockSpec((None, BLOCK, qw), lambda b, i, g: (b, i, g)),
            band(k_col, -1), band(k_col, 0), band(k_col, 1),
            band(v_col, -1), band(v_col, 0), band(v_col, 1),
            pl.BlockSpec((None, BLOCK, qw), lambda b, i, g: (b, i, ga_col + g)),
            pl.BlockSpec((1, qw), lambda b, i, g: (0, g)),
        ],
        out_specs=pl.BlockSpec((None, BLOCK, qw), lambda b, i, g: (b, i, g)),
        out_shape=jax.ShapeDtypeStruct((batch, seq, ATTN_WIDTH), F32),
        compiler_params=_params(("parallel", "parallel", "parallel")),
        name="attention",
    )(head_params, qkv3, qkv3, qkv3, qkv3, qkv3, qkv3, qkv3, rest3, gate_bias)
    return out.reshape(batch * seq, ATTN_WIDTH)


def _conv_kernel(prev_ref, cur_ref, next_ref, w_ref, b_ref, o_ref, ext_ref):
    i = pl.program_id(1)
    tb = cur_ref.shape[0]
    zeros = jnp.zeros(prev_ref.shape, F32)
    ext_ref[0:SUBLANES, :] = jnp.where(i > 0, prev_ref[...], zeros)
    ext_ref[SUBLANES:SUBLANES + tb, :] = cur_ref[...]
    ext_ref[SUBLANES + tb:2 * SUBLANES + tb, :] = jnp.where(i < pl.num_programs(1) - 1, next_ref[...], zeros)
    acc = jnp.zeros(cur_ref.shape, F32)
    for k in range(CONV_WIDTH):
        acc = acc + w_ref[k:k + 1, :] * ext_ref[pl.ds(SUBLANES - CONV_HALF + k, tb), :]
    acc = acc + b_ref[...]
    o_ref[...] = acc * jax.nn.sigmoid(acc)


def _conv_silu(rest, conv_w, conv_b, batch, seq, tb=512, tc=512):
    tb = min(tb, seq)
    rest3 = rest.reshape(batch, seq, REST_WIDTH)
    nt = seq // tb
    col0 = REST_XBC // tc
    sub_per_blk = tb // SUBLANES
    last_sub = seq // SUBLANES - 1
    out = pl.pallas_call(
        _conv_kernel,
        grid=(batch, nt, XBC_WIDTH // tc),
        in_specs=[
            pl.BlockSpec((None, SUBLANES, tc),
                         lambda b, i, c: (b, jnp.maximum(i * sub_per_blk - 1, 0), col0 + c)),
            pl.BlockSpec((None, tb, tc), lambda b, i, c: (b, i, col0 + c)),
            pl.BlockSpec((None, SUBLANES, tc),
                         lambda b, i, c: (b, jnp.minimum((i + 1) * sub_per_blk, last_sub), col0 + c)),
            pl.BlockSpec((CONV_WIDTH, tc), lambda b, i, c: (0, c)),
            pl.BlockSpec((1, tc), lambda b, i, c: (0, c)),
        ],
        out_specs=pl.BlockSpec((None, tb, tc), lambda b, i, c: (b, i, c)),
        out_shape=jax.ShapeDtypeStruct((batch, seq, XBC_WIDTH), F32),
        scratch_shapes=[pltpu.VMEM((tb + 2 * SUBLANES, tc), F32)],
        compiler_params=_params(("parallel", "parallel", "parallel")),
        name="conv_silu",
    )(rest3, rest3, rest3, conv_w, conv_b)
    return out.reshape(batch * seq, XBC_WIDTH)


def _split3(x):
    hi = x.astype(BF16)
    r1 = x - hi.astype(F32)
    mid = r1.astype(BF16)
    lo = (r1 - mid.astype(F32)).astype(BF16)
    return hi, mid, lo


def _dt_kernel(raw_ref, bias_ref, alog_ref, dtt_ref, cumt_ref):
    x = raw_ref[...] + bias_ref[...]
    dt = jnp.maximum(x, 0.0) + jnp.log1p(jnp.exp(-jnp.abs(x)))
    da = dt * (-jnp.exp(alog_ref[...]))
    t_idx = lax.broadcasted_iota(jnp.int32, (CHUNK, CHUNK), 0)
    s_idx = lax.broadcasted_iota(jnp.int32, (CHUNK, CHUNK), 1)
    lower = (s_idx <= t_idx).astype(BF16)
    upper = (s_idx >= t_idx).astype(BF16)
    cum_f = jnp.zeros(da.shape, F32)
    cum_r = jnp.zeros(da.shape, F32)
    for piece in _split3(da):
        cum_f = cum_f + jnp.dot(lower, piece, preferred_element_type=F32)
        cum_r = cum_r + jnp.dot(upper, piece, preferred_element_type=F32)
    col = lax.broadcasted_iota(jnp.int32, da.shape, 1)
    cum = jnp.where(col < SSM_HEADS, cum_f, cum_r)
    dtt_ref[...] = dt.T
    cumt_ref[...] = cum.T


def _dt_prep(rest, dt_bias, a_log):
    m = rest.shape[0]
    nchunks = m // CHUNK
    w = 2 * SSM_HEADS
    return pl.pallas_call(
        _dt_kernel,
        grid=(nchunks,),
        in_specs=[
            pl.BlockSpec((CHUNK, w), lambda c: (c, REST_DT // w)),
            pl.BlockSpec((1, w), lambda c: (0, 0)),
            pl.BlockSpec((1, w), lambda c: (0, 0)),
        ],
        out_specs=[
            pl.BlockSpec((w, CHUNK), lambda c: (c, 0)),
            pl.BlockSpec((w, CHUNK), lambda c: (c, 0)),
        ],
        out_shape=[
            jax.ShapeDtypeStruct((nchunks * w, CHUNK), F32),
            jax.ShapeDtypeStruct((nchunks * w, CHUNK), F32),
        ],
        compiler_params=_params(("parallel",)),
        name="dt_prep",
    )(rest, dt_bias, a_log)


def _ssd_group(xs, bm, cm, dtt, cumt, state, reverse):
    hg = HEADS_PER_GROUP
    p = SSM_HEAD_DIM
    tot = cumt[:, 0:1] if reverse else cumt[:, CHUNK - 1:CHUNK]

    narrow = jnp.concatenate(
        [dtt * jnp.exp(tot - cumt), jnp.exp(cumt), jnp.broadcast_to(jnp.exp(tot), (hg, CHUNK))], axis=1)
    narrow = jnp.concatenate([narrow, jnp.zeros((BF16_SUBLANES - hg, 3 * CHUNK), F32)], axis=0)
    head_of_lane = lax.broadcasted_iota(jnp.int32, (BF16_SUBLANES, GROUP_WIDTH), 1) // p
    expand = (head_of_lane == lax.broadcasted_iota(jnp.int32, (BF16_SUBLANES, GROUP_WIDTH), 0)).astype(BF16)
    wide = jnp.zeros((3 * CHUNK, GROUP_WIDTH), F32)
    for piece in _split3(narrow):
        wide = wide + lax.dot_general(piece, expand, (((0,), (0,)), ((), ())), preferred_element_type=F32)
    todecay_w = wide[0:CHUNK]
    ecum_w = wide[CHUNK:2 * CHUNK]
    etot_w = wide[2 * CHUNK:2 * CHUNK + 1]
    xb = xs.astype(BF16)
    xend = (xs * todecay_w).astype(BF16)

    cb = lax.dot_general(cm, bm, (((1,), (1,)), ((), ())), preferred_element_type=F32)
    y_off =jnp.dot(cm, state.astype(BF16), preferred_element_type=F32) * ecum_w

    cum_col = jnp.concatenate([cumt, jnp.zeros((CHUNK - hg, CHUNK), F32)], axis=0).T
    t_idx = lax.broadcasted_iota(jnp.int32, (CHUNK, CHUNK), 0)
    s_idx = lax.broadcasted_iota(jnp.int32, (CHUNK, CHUNK), 1)
    mask = (s_idx >= t_idx) if reverse else (s_idx <= t_idx)
    lane = lax.broadcasted_iota(jnp.int32, (CHUNK, 2 * p), 1)
    y_diag = []
    for pair in range(hg // 2):
        mats = []
        for h in (2 * pair, 2 * pair + 1):
            seg = cum_col[:, h:h + 1] - cumt[h:h + 1, :]
            decay = jnp.exp(jnp.where(mask, seg, -jnp.inf))
            mats.append((cb * decay * dtt[h:h + 1, :]).astype(BF16))
        lhs = jnp.concatenate(mats, axis=1)
        xp = xb[:, pair * 2 * p:(pair + 1) * 2 * p]
        zero = jnp.zeros_like(xp)
        rhs = jnp.concatenate([jnp.where(lane < p, xp, zero), jnp.where(lane >= p, xp, zero)], axis=0)
        y_diag.append(jnp.dot(lhs, rhs, preferred_element_type=F32))
    y = y_off + jnp.concatenate(y_diag, axis=1)

    new_state = lax.dot_general(bm, xend, (((0,), (0,)), ((), ())), preferred_element_type=F32)
    return y, state * etot_w + new_state


def _ssd_kernel(xs_ref, b_ref, c_ref, dtt_ref, cumt_ref, *rest, reverse, finish):
    if finish:
        yb_ref, z_ref, gb_ref, att_ref, dsk_ref, nw_ref, bb_ref, o_ref, state_ref = rest
    else:
        o_ref, state_ref = rest
    hg = HEADS_PER_GROUP

    @pl.when(pl.program_id(2) == 0)
    def _():
        state_ref[...] = jnp.zeros_like(state_ref)

    for gi in range(state_ref.shape[0]):
        cols = slice(gi * GROUP_WIDTH, (gi + 1) * GROUP_WIDTH)
        ncols = slice(gi * SSM_STATE, (gi + 1) * SSM_STATE)
        heads = slice(gi * hg, (gi + 1) * hg)
        xs = xs_ref[:, cols]
        y, state = _ssd_group(xs, b_ref[:, ncols].astype(BF16), c_ref[:, ncols].astype(BF16),
                              dtt_ref[heads, :], cumt_ref[heads, :], state_ref[gi], reverse)
        state_ref[gi] = state
        if finish:
            z = z_ref[:, cols]
            y = (y + yb_ref[:, cols] + dsk_ref[:, cols] * xs) * (z * jax.nn.sigmoid(z))
            ms = jnp.mean(y * y, axis=-1, keepdims=True)
            ssm = y * lax.rsqrt(ms + RMS_EPS) * nw_ref[:, cols]
            gate = jax.nn.sigmoid(gb_ref[:, cols] + bb_ref[:, cols])
            o_ref[:, cols] = (att_ref[:, cols] + gate * ssm).astype(o_ref.dtype)
        else:
            o_ref[:, cols] = y


SSD_GROUPS_PER_STEP = 8


def _ssd(xa, dtt, cumt, batch, seq, reverse, finish_args=None):
    m = batch * seq
    nc = seq // CHUNK
    gps = SSD_GROUPS_PER_STEP
    gw = gps * GROUP_WIDTH
    nw = gps * SSM_STATE
    assert SSM_INNER % nw == 0 and (SSM_GROUPS * SSM_STATE) % nw == 0
    assert REST_Z % gw == 0 and (REST_GATE + D_MODEL) % gw == 0 and D_MODEL % gw == 0
    b_col = SSM_INNER // nw
    c_col = b_col + SSM_GROUPS // gps
    blocks_per_chunk = 2 * SSM_GROUPS // gps
    direction = 1 if reverse else 0

    def row(b, c):
        return b * nc + ((nc - 1 - c) if reverse else c)

    def blk(width, col0):
        return pl.BlockSpec((CHUNK, width), lambda b, g, c: (row(b, c), col0 + g))

    def small():
        return pl.BlockSpec((gps * HEADS_PER_GROUP, CHUNK),
                            lambda b, g, c: (row(b, c) * blocks_per_chunk + direction * (SSM_GROUPS // gps) + g, 0))

    def vec(col0):
        return pl.BlockSpec((1, gw), lambda b, g, c: (0, col0 + g))

    in_specs = [blk(gw, 0), blk(nw, b_col), blk(nw, c_col), small(), small()]
    args = [xa, xa, xa, dtt, cumt]
    finish = finish_args is not None
    if finish:
        y_bwd, rest, gated_attn, d_skip_wide, norm_w, gate_bias = finish_args
        in_specs += [blk(gw, 0), blk(gw, REST_Z // gw), blk(gw, (REST_GATE + D_MODEL) // gw), blk(gw, 0),
                     vec(0), vec(0), vec(D_MODEL // gw)]
        args += [y_bwd, rest, rest, gated_attn, d_skip_wide, norm_w, gate_bias]
    return pl.pallas_call(
        functools.partial(_ssd_kernel, reverse=reverse, finish=finish),
        grid=(batch, SSM_GROUPS // gps, nc),
        in_specs=in_specs,
        out_specs=blk(gw, 0),
        out_shape=jax.ShapeDtypeStruct((m, SSM_INNER), BF16 if finish else F32),
        scratch_shapes=[pltpu.VMEM((gps, SSM_STATE, GROUP_WIDTH), F32)],
        compiler_params=_params(("parallel", "parallel", "arbitrary")),
        name="ssd_fwd_merge" if finish else "ssd_bwd",
    )(*args)


def _proj_ln_kernel(h_ref, w_ref, x_ref, g_ref, b_ref, o_ref):
    j = pl.program_id(1)

    @pl.when(j == 0)
    def _():
        o_ref[...] = jnp.zeros_like(o_ref)

    o_ref[...] += jnp.dot(h_ref[...], w_ref[...], preferred_element_type=F32)

    @pl.when(j == pl.num_programs(1) - 1)
    def _():
        _residual_ln_inplace(x_ref, o_ref, g_ref, b_ref, 1.0)


def _proj_ln(h, w, x, ln_g, ln_b, tm=512, tk=1024):
    m, k = h.shape
    tm = min(tm, m)
    d = w.shape[1]
    return pl.pallas_call(
        _proj_ln_kernel,
        grid=(m // tm, k // tk),
        in_specs=[
            pl.BlockSpec((tm, tk), lambda i, j: (i, j)),
            pl.BlockSpec((tk, d), lambda i, j: (j, 0)),
            pl.BlockSpec((tm, d), lambda i, j: (i, 0), pipeline_mode=pl.Buffered(1)),
            pl.BlockSpec((1, d), lambda i, j: (0, 0)),
            pl.BlockSpec((1, d), lambda i, j: (0, 0)),
        ],
        out_specs=pl.BlockSpec((tm, d), lambda i, j: (i, 0)),
        out_shape=jax.ShapeDtypeStruct((m, d), F32),
        compiler_params=_params(("parallel", "arbitrary")),
        name="proj_ln",
    )(h, w, x, ln_g, ln_b)


def _prepare(ffn1_w_gate, ffn1_w_up, ffn1_w_down, ln1_g, ln1_b, w_in, conv_w, conv_b, dt_bias, a_log,
             d_skip, ssm_norm_w, sink, gate_bias, w_out, ln2_g, ln2_b,
             ffn2_w_gate, ffn2_w_up, ffn2_w_down, ln3_g, ln3_b):
    row = lambda v: v.reshape(1, -1).astype(F32)
    dt0 = QKV_WIDTH + SSM_INNER + XBC_WIDTH
    dt1 = dt0 + 2 * SSM_HEADS
    z1 = QKV_WIDTH + SSM_INNER
    w_rest = jnp.concatenate(
        [w_in[:, QKV_WIDTH:z1], w_in[:, dt1:], w_in[:, z1:dt0], w_in[:, dt0:dt1],
         jnp.zeros((D_MODEL, REST_WIDTH - (REST_DT + 2 * SSM_HEADS)), w_in.dtype)], axis=1)
    slopes = 2.0 ** (-8.0 * jnp.arange(1, N_HEADS + 1, dtype=F32) / N_HEADS)
    return dict(
        ffn1=(ffn1_w_gate.astype(BF16), ffn1_w_up.astype(BF16), ffn1_w_down.astype(BF16), row(ln1_g), row(ln1_b)),
        w_qkv=w_in[:, :QKV_WIDTH].astype(BF16),
        w_rest=w_rest.astype(BF16),
        conv_w=conv_w.astype(F32), conv_b=row(conv_b),
        dt_bias=row(dt_bias), a_log=row(a_log),
        d_skip_wide=row(jnp.repeat(d_skip.astype(F32), SSM_HEAD_DIM)),
        norm_w=row(ssm_norm_w),
        head_params=jnp.stack([slopes, sink.astype(F32)]),
        gate_bias=row(gate_bias),
        w_out=w_out.astype(BF16), ln2=(row(ln2_g), row(ln2_b)),
        ffn2=(ffn2_w_gate.astype(BF16), ffn2_w_up.astype(BF16), ffn2_w_down.astype(BF16), row(ln3_g), row(ln3_b)),
    )


def _layer(x, p):
    batch, seq, d = x.shape
    x2d = x.reshape(batch * seq, d)
    x1, x1b = _ffn_ln(x2d, *p["ffn1"], emit_bf16=True)
    qkv = _matmul(x1b, p["w_qkv"], BF16, name="proj_qkv")
    rest = _matmul(x1b, p["w_rest"], F32, name="proj_rest")
    gated_attn = _attention(qkv, rest, p["head_params"], p["gate_bias"], batch, seq)
    xa = _conv_silu(rest, p["conv_w"], p["conv_b"], batch, seq)
    dtt, cumt = _dt_prep(rest, p["dt_bias"], p["a_log"])
    y_bwd = _ssd(xa, dtt, cumt, batch, seq, reverse=True)
    h = _ssd(xa, dtt, cumt, batch, seq, reverse=False,
             finish_args=(y_bwd, rest, gated_attn, p["d_skip_wide"], p["norm_w"], p["gate_bias"]))
    x2 = _proj_ln(h, p["w_out"], x1, *p["ln2"])
    y = _ffn_ln(x2, *p["ffn2"], emit_bf16=False)
    return y.reshape(batch, seq, d)


def kernel(x_prompt, x_sample, ffn1_w_gate, ffn1_w_up, ffn1_w_down, ln1_g, ln1_b, w_in, conv_w, conv_b, dt_bias, a_log, d_skip, ssm_norm_w, sink, gate_bias, w_out, ln2_g, ln2_b, ffn2_w_gate, ffn2_w_up, ffn2_w_down, ln3_g, ln3_b):
    weights = (ffn1_w_gate, ffn1_w_up, ffn1_w_down, ln1_g, ln1_b, w_in, conv_w, conv_b, dt_bias, a_log,
               d_skip, ssm_norm_w, sink, gate_bias, w_out, ln2_g, ln2_b,
               ffn2_w_gate, ffn2_w_up, ffn2_w_down, ln3_g, ln3_b)
    p = _prepare(*[w[0] for w in weights])
    return (_layer(x_prompt, p), _layer(x_sample, p))
```

```python
import functools

import jax
import jax.numpy as jnp
from jax import lax
from jax.experimental import pallas as pl
from jax.experimental.pallas import tpu as pltpu

F32 = jnp.float32
BF16 = jnp.bfloat16

D_MODEL = 4096
DEPTH = 1
N_HEADS = 32
N_KV_HEADS = 8
HEAD_DIM = 128
KV_REP = N_HEADS // N_KV_HEADS
ATTN_WIDTH = N_HEADS * HEAD_DIM
KV_WIDTH = N_KV_HEADS * HEAD_DIM
WINDOW = 128
BLOCK = 128
SSM_HEAD_DIM = 64
SSM_HEADS = 64
SSM_INNER = 4096
SSM_GROUPS = 8
SSM_STATE = 128
HEADS_PER_GROUP = SSM_HEADS // SSM_GROUPS
GROUP_WIDTH = HEADS_PER_GROUP * SSM_HEAD_DIM
XBC_WIDTH = SSM_INNER + 2 * SSM_GROUPS * SSM_STATE
CONV_WIDTH = 5
CONV_HALF = CONV_WIDTH // 2
CHUNK = 128
D_FF = 11008
ALPHA = (2 * DEPTH) ** 0.25
LN_EPS = 1e-5
RMS_EPS = 1e-5
LOG2E = 1.4426950408889634

SUBLANES = 8
BF16_SUBLANES = 16
VMEM_LIMIT = 56 * 1024 * 1024

REST_Z = 0
REST_GATE = SSM_INNER
REST_XBC = REST_GATE + 2 * D_MODEL
REST_DT = REST_XBC + XBC_WIDTH
REST_TILE = 512
REST_WIDTH = ((REST_DT + 2 * SSM_HEADS + REST_TILE - 1) // REST_TILE) * REST_TILE
QKV_WIDTH = ATTN_WIDTH + 2 * KV_WIDTH


def _params(semantics):
    return pltpu.CompilerParams(dimension_semantics=semantics, vmem_limit_bytes=VMEM_LIMIT)


def _layer_norm_rows(y, g, b):
    mu = jnp.mean(y, axis=-1, keepdims=True)
    yc = y - mu
    var = jnp.mean(yc * yc, axis=-1, keepdims=True)
    return yc * lax.rsqrt(var + LN_EPS) * g + b


def _residual_ln_inplace(x_ref, o_ref, g_ref, b_ref, branch_scale, rows=64):
    g = g_ref[...]
    b = b_ref[...]

    def body(r, carry):
        sl = pl.ds(pl.multiple_of(r * rows, rows), rows)
        y = ALPHA * x_ref[sl, :] + branch_scale * o_ref[sl, :]
        o_ref[sl, :] = _layer_norm_rows(y, g, b)
        return carry

    lax.fori_loop(0, o_ref.shape[0] // rows, body, 0)


ROW_TILE = 1024


def _ffn_ln_kernel(x_ref, wg_ref, wu_ref, wd_ref, g_ref, b_ref, o_ref, xb_ref):
    j = pl.program_id(1)

    @pl.when(j == 0)
    def _():
        xb_ref[...] = x_ref[...].astype(BF16)
        o_ref[...] = jnp.zeros_like(o_ref)

    xb = xb_ref[...]
    gate = jnp.dot(xb, wg_ref[...], preferred_element_type=F32)
    up = jnp.dot(xb, wu_ref[...], preferred_element_type=F32)
    h = (gate * jax.nn.sigmoid(gate) * up).astype(BF16)
    o_ref[...] += jnp.dot(h, wd_ref[...], preferred_element_type=F32)

    @pl.when(j == pl.num_programs(1) - 1)
    def _():
        _residual_ln_inplace(x_ref, o_ref, g_ref, b_ref, 0.5)


def _ffn_ln(x, wg, wu, wd, ln_g, ln_b, tm=ROW_TILE, tf=256):
    m, d = x.shape
    tm = min(tm, m)
    dff = wg.shape[1]
    return pl.pallas_call(
        _ffn_ln_kernel,
        grid=(m // tm, dff // tf),
        in_specs=[
            pl.BlockSpec((tm, d), lambda i, j: (i, 0), pipeline_mode=pl.Buffered(1)),
            pl.BlockSpec((d, tf), lambda i, j: (0, j)),
            pl.BlockSpec((d, tf), lambda i, j: (0, j)),
            pl.BlockSpec((tf, d), lambda i, j: (j, 0)),
            pl.BlockSpec((1, d), lambda i, j: (0, 0)),
            pl.BlockSpec((1, d), lambda i, j: (0, 0)),
        ],
        out_specs=pl.BlockSpec((tm, d), lambda i, j: (i, 0), pipeline_mode=pl.Buffered(1)),
        out_shape=jax.ShapeDtypeStruct((m, d), F32),
        scratch_shapes=[pltpu.VMEM((tm, d), BF16)],
        compiler_params=_params(("parallel", "arbitrary")),
        name="ffn_ln",
    )(x, wg, wu, wd, ln_g, ln_b)


def _matmul_kernel(x_ref, w_ref, o_ref, xb_ref):
    @pl.when(pl.program_id(1) == 0)
    def _():
        xb_ref[...] = x_ref[...].astype(BF16)

    o_ref[...] = jnp.dot(xb_ref[...], w_ref[...], preferred_element_type=F32).astype(o_ref.dtype)


def _matmul(x, w, out_dtype, tm=ROW_TILE, tn=512, name="matmul"):
    m, k = x.shape
    tm = min(tm, m)
    n = w.shape[1]
    return pl.pallas_call(
        _matmul_kernel,
        grid=(m // tm, n // tn),
        in_specs=[
            pl.BlockSpec((tm, k), lambda i, j: (i, 0)),
            pl.BlockSpec((k, tn), lambda i, j: (0, j)),
        ],
        out_specs=pl.BlockSpec((tm, tn), lambda i, j: (i, j)),
        out_shape=jax.ShapeDtypeStruct((m, n), out_dtype),
        scratch_shapes=[pltpu.VMEM((tm, k), BF16)],
        compiler_params=_params(("parallel", "arbitrary")),
        name=name,
    )(x, w)


def _attn_kernel(hp_ref, q_ref, kp_ref, kc_ref, kn_ref, vp_ref, vc_ref, vn_ref, ga_ref, ba_ref, o_ref):
    i = pl.program_id(1)
    nb = pl.num_programs(1)
    kv_heads = kc_ref.shape[1] // HEAD_DIM

    t_idx = lax.broadcasted_iota(jnp.int32, (BLOCK, 3 * BLOCK), 0)
    s_idx = lax.broadcasted_iota(jnp.int32, (BLOCK, 3 * BLOCK), 1)
    dist = jnp.abs(t_idx - s_idx + BLOCK)
    in_seq = ((s_idx >= BLOCK) | (i > 0)) & ((s_idx < 2 * BLOCK) | (i < nb - 1))
    valid = (dist <= WINDOW) & in_seq
    dist_f = dist.astype(F32)

    for kv in range(kv_heads):
        kcols = slice(kv * HEAD_DIM, (kv + 1) * HEAD_DIM)
        q0 = kv * KV_REP * HEAD_DIM
        qs = jnp.concatenate([q_ref[:, q0 + r * HEAD_DIM:q0 + (r + 1) * HEAD_DIM] for r in range(KV_REP)], axis=0)
        k = jnp.concatenate([kp_ref[:, kcols], kc_ref[:, kcols], kn_ref[:, kcols]], axis=0)
        v = jnp.concatenate([vp_ref[:, kcols], vc_ref[:, kcols], vn_ref[:, kcols]], axis=0)
        s = lax.dot_general(qs, k, (((1,), (1,)), ((), ())), preferred_element_type=F32)
        s = s * (HEAD_DIM ** -0.5 * LOG2E)
        probs = []
        for r in range(KV_REP):
            head = (pl.program_id(2) * kv_heads + kv) * KV_REP + r
            slope = hp_ref[0, head] * LOG2E
            sink = hp_ref[1, head] * LOG2E
            sc = s[r * BLOCK:(r + 1) * BLOCK] - slope * dist_f
            sc = jnp.where(valid, sc, -1e30)
            m = jnp.maximum(jnp.max(sc, axis=-1, keepdims=True), sink)
            p = jnp.exp2(sc - m)
            denom = jnp.sum(p, axis=-1, keepdims=True) + jnp.exp2(sink - m)
            probs.append((p * (1.0 / denom)).astype(BF16))
        pn = jnp.concatenate(probs, axis=0)
        out = jnp.dot(pn, v, preferred_element_type=F32)
        for r in range(KV_REP):
            cols = slice(q0 + r * HEAD_DIM, q0 + (r + 1) * HEAD_DIM)
            gate = jax.nn.sigmoid(ga_ref[:, cols] + ba_ref[:, cols])
            o_ref[:, cols] = gate * out[r * BLOCK:(r + 1) * BLOCK]


ATTN_KV_PER_STEP = 8


def _attention(qkv, rest, head_params, gate_bias, batch, seq):
    nb = seq // BLOCK
    qkv3 = qkv.reshape(batch, seq, QKV_WIDTH)
    rest3 = rest.reshape(batch, seq, REST_WIDTH)
    kvw = ATTN_KV_PER_STEP * HEAD_DIM
    k_col = ATTN_WIDTH // kvw
    v_col = (ATTN_WIDTH + KV_WIDTH) // kvw
    qw = ATTN_KV_PER_STEP * KV_REP * HEAD_DIM
    assert REST_GATE % qw == 0 and ATTN_WIDTH % kvw == 0 and KV_WIDTH % kvw == 0
    ga_col = REST_GATE // qw

    def band(col0, shift):
        def index(b, i, g):
            return (b, jnp.clip(i + shift, 0, nb - 1), col0 + g)
        return pl.BlockSpec((None, BLOCK, kvw), index)

    out = pl.pallas_call(
        _attn_kernel,
        grid=(batch, nb, N_KV_HEADS // ATTN_KV_PER_STEP),
        in_specs=[
            pl.BlockSpec(memory_space=pltpu.SMEM),
            pl.BlockSpec((None, BLOCK, qw), lambda b, i, g: (b, i, g)),
            band(k_col, -1), band(k_col, 0), band(k_col, 1),
            band(v_col, -1), band(v_col, 0), band(v_col, 1),
            pl.BlockSpec((None, BLOCK, qw), lambda b, i, g: (b, i, ga_col + g)),
            pl.BlockSpec((1, qw), lambda b, i, g: (0, g)),
        ],
        out_specs=pl.BlockSpec((None, BLOCK, qw), lambda b, i, g: (b, i, g)),
        out_shape=jax.ShapeDtypeStruct((batch, seq, ATTN_WIDTH), F32),
        compiler_params=_params(("parallel", "parallel", "parallel")),
        name="attention",
    )(head_params, qkv3, qkv3, qkv3, qkv3, qkv3, qkv3, qkv3, rest3, gate_bias)
    return out.reshape(batch * seq, ATTN_WIDTH)


def _conv_kernel(prev_ref, cur_ref, next_ref, w_ref, b_ref, o_ref, ext_ref):
    i = pl.program_id(1)
    tb = cur_ref.shape[0]
    zeros = jnp.zeros(prev_ref.shape, F32)
    ext_ref[0:SUBLANES, :] = jnp.where(i > 0, prev_ref[...], zeros)
    ext_ref[SUBLANES:SUBLANES + tb, :] = cur_ref[...]
    ext_ref[SUBLANES + tb:2 * SUBLANES + tb, :] = jnp.where(i < pl.num_programs(1) - 1, next_ref[...], zeros)
    ext = ext_ref[...]
    rows = ext.shape[0]
    acc = jnp.zeros(cur_ref.shape, F32)
    for k in range(CONV_WIDTH):
        rolled = ext if k == CONV_HALF else pltpu.roll(ext, (CONV_HALF - k) % rows, 0)
        acc = acc + w_ref[k:k + 1, :] * rolled[SUBLANES:SUBLANES + tb]
    acc = acc + b_ref[...]
    o_ref[...] = acc * jax.nn.sigmoid(acc)


def _conv_silu(rest, conv_w, conv_b, batch, seq, tb=512, tc=512):
    tb = min(tb, seq)
    rest3 = rest.reshape(batch, seq, REST_WIDTH)
    nt = seq // tb
    col0 = REST_XBC // tc
    sub_per_blk = tb // SUBLANES
    last_sub = seq // SUBLANES - 1
    out = pl.pallas_call(
        _conv_kernel,
        grid=(batch, nt, XBC_WIDTH // tc),
        in_specs=[
            pl.BlockSpec((None, SUBLANES, tc),
                         lambda b, i, c: (b, jnp.maximum(i * sub_per_blk - 1, 0), col0 + c)),
            pl.BlockSpec((None, tb, tc), lambda b, i, c: (b, i, col0 + c)),
            pl.BlockSpec((None, SUBLANES, tc),
                         lambda b, i, c: (b, jnp.minimum((i + 1) * sub_per_blk, last_sub), col0 + c)),
            pl.BlockSpec((CONV_WIDTH, tc), lambda b, i, c: (0, c)),
            pl.BlockSpec((1, tc), lambda b, i, c: (0, c)),
        ],
        out_specs=pl.BlockSpec((None, tb, tc), lambda b, i, c: (b, i, c)),
        out_shape=jax.ShapeDtypeStruct((batch, seq, XBC_WIDTH), F32),
        scratch_shapes=[pltpu.VMEM((tb + 2 * SUBLANES, tc), F32)],
        compiler_params=_params(("parallel", "parallel", "parallel")),
        name="conv_silu",
    )(rest3, rest3, rest3, conv_w, conv_b)
    return out.reshape(batch * seq, XBC_WIDTH)


def _split3(x):
    hi = x.astype(BF16)
    r1 = x - hi.astype(F32)
    mid = r1.astype(BF16)
    lo = (r1 - mid.astype(F32)).astype(BF16)
    return hi, mid, lo


def _dt_kernel(raw_ref, bias_ref, alog_ref, dtt_ref, cumt_ref):
    x = raw_ref[...] + bias_ref[...]
    dt = jnp.maximum(x, 0.0) + jnp.log1p(jnp.exp(-jnp.abs(x)))
    da = dt * (-jnp.exp(alog_ref[...]))
    t_idx = lax.broadcasted_iota(jnp.int32, (CHUNK, CHUNK), 0)
    s_idx = lax.broadcasted_iota(jnp.int32, (CHUNK, CHUNK), 1)
    lower = (s_idx <= t_idx).astype(BF16)
    upper = (s_idx >= t_idx).astype(BF16)
    cum_f = jnp.zeros(da.shape, F32)
    cum_r = jnp.zeros(da.shape, F32)
    for piece in _split3(da):
        cum_f = cum_f + jnp.dot(lower, piece, preferred_element_type=F32)
        cum_r = cum_r + jnp.dot(upper, piece, preferred_element_type=F32)
    col = lax.broadcasted_iota(jnp.int32, da.shape, 1)
    cum = jnp.where(col < SSM_HEADS, cum_f, cum_r)
    dtt_ref[...] = dt.T
    cumt_ref[...] = cum.T


def _dt_prep(rest, dt_bias, a_log):
    m = rest.shape[0]
    nchunks = m // CHUNK
    w = 2 * SSM_HEADS
    return pl.pallas_call(
        _dt_kernel,
        grid=(nchunks,),
        in_specs=[
            pl.BlockSpec((CHUNK, w), lambda c: (c, REST_DT // w)),
            pl.BlockSpec((1, w), lambda c: (0, 0)),
            pl.BlockSpec((1, w), lambda c: (0, 0)),
        ],
        out_specs=[
            pl.BlockSpec((w, CHUNK), lambda c: (c, 0)),
            pl.BlockSpec((w, CHUNK), lambda c: (c, 0)),
        ],
        out_shape=[
            jax.ShapeDtypeStruct((nchunks * w, CHUNK), F32),
            jax.ShapeDtypeStruct((nchunks * w, CHUNK), F32),
        ],
        compiler_params=_params(("parallel",)),
        name="dt_prep",
    )(rest, dt_bias, a_log)


def _ssd_group(xs, bm, cm, dtt, cumt, state, reverse):
    hg = HEADS_PER_GROUP
    p = SSM_HEAD_DIM
    tot = cumt[:, 0:1] if reverse else cumt[:, CHUNK - 1:CHUNK]

    narrow = jnp.concatenate([dtt * jnp.exp(tot - cumt), jnp.exp(cumt)], axis=1)
    narrow = jnp.concatenate([narrow, jnp.zeros((BF16_SUBLANES - hg, 2 * CHUNK), F32)], axis=0)
    pieces = jnp.concatenate(_split3(narrow), axis=0)
    head_of_lane = lax.broadcasted_iota(jnp.int32, (3 * BF16_SUBLANES, GROUP_WIDTH), 1) // p
    piece_row = lax.broadcasted_iota(jnp.int32, (3 * BF16_SUBLANES, GROUP_WIDTH), 0) % BF16_SUBLANES
    expand = (head_of_lane == piece_row).astype(BF16)
    wide = lax.dot_general(pieces, expand, (((0,), (0,)), ((), ())), preferred_element_type=F32)
    todecay_w = wide[0:CHUNK]
    ecum_w = wide[CHUNK:2 * CHUNK]
    etot_w = ecum_w[0:1] if reverse else ecum_w[CHUNK - 1:CHUNK]
    xb = xs.astype(BF16)
    xend = (xs * todecay_w).astype(BF16)

    cb = lax.dot_general(cm, bm, (((1,), (1,)), ((), ())), preferred_element_type=F32)
    y_off =jnp.dot(cm, state.astype(BF16), preferred_element_type=F32) * ecum_w

    cum2 = cumt * LOG2E
    cum2_col = jnp.concatenate([cum2, jnp.zeros((CHUNK - hg, CHUNK), F32)], axis=0).T
    t_idx = lax.broadcasted_iota(jnp.int32, (CHUNK, CHUNK), 0)
    s_idx = lax.broadcasted_iota(jnp.int32, (CHUNK, CHUNK), 1)
    mask = (s_idx >= t_idx) if reverse else (s_idx <= t_idx)
    lane = lax.broadcasted_iota(jnp.int32, (CHUNK, 2 * p), 1)
    y_diag = []
    for pair in range(hg // 2):
        mats = []
        for h in (2 * pair, 2 * pair + 1):
            seg2 = cum2_col[:, h:h + 1] - cum2[h:h + 1, :]
            decay = jnp.exp2(jnp.where(mask, seg2, -jnp.inf))
            mats.append((cb * decay * dtt[h:h + 1, :]).astype(BF16))
        lhs = jnp.concatenate(mats, axis=1)
        xp = xb[:, pair * 2 * p:(pair + 1) * 2 * p]
        zero = jnp.zeros_like(xp)
        rhs = jnp.concatenate([jnp.where(lane < p, xp, zero), jnp.where(lane >= p, xp, zero)], axis=0)
        y_diag.append(jnp.dot(lhs, rhs, preferred_element_type=F32))
    y = y_off + jnp.concatenate(y_diag, axis=1)

    new_state = lax.dot_general(bm, xend, (((0,), (0,)), ((), ())), preferred_element_type=F32)
    return y, state * etot_w + new_state


def _ssd_kernel(xs_ref, b_ref, c_ref, dtt_ref, cumt_ref, *rest, reverse, finish):
    if finish:
        yb_ref, z_ref, gb_ref, att_ref, dsk_ref, nw_ref, bb_ref, o_ref, state_ref = rest
    else:
        o_ref, state_ref = rest
    hg = HEADS_PER_GROUP

    @pl.when(pl.program_id(2) == 0)
    def _():
        state_ref[...] = jnp.zeros_like(state_ref)

    for gi in range(state_ref.shape[0]):
        cols = slice(gi * GROUP_WIDTH, (gi + 1) * GROUP_WIDTH)
        ncols = slice(gi * SSM_STATE, (gi + 1) * SSM_STATE)
        heads = slice(gi * hg, (gi + 1) * hg)
        xs = xs_ref[:, cols]
        y, state = _ssd_group(xs, b_ref[:, ncols].astype(BF16), c_ref[:, ncols].astype(BF16),
                              dtt_ref[heads, :], cumt_ref[heads, :], state_ref[gi], reverse)
        state_ref[gi] = state
        if finish:
            z = z_ref[:, cols]
            y = (y + yb_ref[:, cols] + dsk_ref[:, cols] * xs) * (z * jax.nn.sigmoid(z))
            ms = jnp.mean(y * y, axis=-1, keepdims=True)
            ssm = y * lax.rsqrt(ms + RMS_EPS) * nw_ref[:, cols]
            gate = jax.nn.sigmoid(gb_ref[:, cols] + bb_ref[:, cols])
            o_ref[:, cols] = (att_ref[:, cols] + gate * ssm).astype(o_ref.dtype)
        else:
            o_ref[:, cols] = y


SSD_GROUPS_PER_STEP = 8


def _ssd(xa, dtt, cumt, batch, seq, reverse, finish_args=None):
    m = batch * seq
    nc = seq // CHUNK
    gps = SSD_GROUPS_PER_STEP
    gw = gps * GROUP_WIDTH
    nw = gps * SSM_STATE
    assert SSM_INNER % nw == 0 and (SSM_GROUPS * SSM_STATE) % nw == 0
    assert REST_Z % gw == 0 and (REST_GATE + D_MODEL) % gw == 0 and D_MODEL % gw == 0
    b_col = SSM_INNER // nw
    c_col = b_col + SSM_GROUPS // gps
    blocks_per_chunk = 2 * SSM_GROUPS // gps
    direction = 1 if reverse else 0

    def row(b, c):
        return b * nc + ((nc - 1 - c) if reverse else c)

    def blk(width, col0):
        return pl.BlockSpec((CHUNK, width), lambda b, g, c: (row(b, c), col0 + g))

    def small():
        return pl.BlockSpec((gps * HEADS_PER_GROUP, CHUNK),
                            lambda b, g, c: (row(b, c) * blocks_per_chunk + direction * (SSM_GROUPS // gps) + g, 0))

    def vec(col0):
        return pl.BlockSpec((1, gw), lambda b, g, c: (0, col0 + g))

    in_specs = [blk(gw, 0), blk(nw, b_col), blk(nw, c_col), small(), small()]
    args = [xa, xa, xa, dtt, cumt]
    finish = finish_args is not None
    if finish:
        y_bwd, rest, gated_attn, d_skip_wide, norm_w, gate_bias = finish_args
        in_specs += [blk(gw, 0), blk(gw, REST_Z // gw), blk(gw, (REST_GATE + D_MODEL) // gw), blk(gw, 0),
                     vec(0), vec(0), vec(D_MODEL // gw)]
        args += [y_bwd, rest, rest, gated_attn, d_skip_wide, norm_w, gate_bias]
    return pl.pallas_call(
        functools.partial(_ssd_kernel, reverse=reverse, finish=finish),
        grid=(batch, SSM_GROUPS // gps, nc),
        in_specs=in_specs,
        out_specs=blk(gw, 0),
        out_shape=jax.ShapeDtypeStruct((m, SSM_INNER), BF16 if finish else F32),
        scratch_shapes=[pltpu.VMEM((gps, SSM_STATE, GROUP_WIDTH), F32)],
        compiler_params=_params(("parallel", "parallel", "arbitrary")),
        name="ssd_fwd_merge" if finish else "ssd_bwd",
    )(*args)


def _proj_ln_kernel(h_ref, w_ref, x_ref, g_ref, b_ref, o_ref):
    j = pl.program_id(1)

    @pl.when(j == 0)
    def _():
        o_ref[...] = jnp.zeros_like(o_ref)

    o_ref[...] += jnp.dot(h_ref[...], w_ref[...], preferred_element_type=F32)

    @pl.when(j == pl.num_programs(1) - 1)
    def _():
        _residual_ln_inplace(x_ref, o_ref, g_ref, b_ref, 1.0)


def _proj_ln(h, w, x, ln_g, ln_b, tm=ROW_TILE, tk=1024):
    m, k = h.shape
    tm = min(tm, m)
    d = w.shape[1]
    return pl.pallas_call(
        _proj_ln_kernel,
        grid=(m // tm, k // tk),
        in_specs=[
            pl.BlockSpec((tm, tk), lambda i, j: (i, j)),
            pl.BlockSpec((tk, d), lambda i, j: (j, 0)),
            pl.BlockSpec((tm, d), lambda i, j: (i, 0), pipeline_mode=pl.Buffered(1)),
            pl.BlockSpec((1, d), lambda i, j: (0, 0)),
            pl.BlockSpec((1, d), lambda i, j: (0, 0)),
        ],
        out_specs=pl.BlockSpec((tm, d), lambda i, j: (i, 0), pipeline_mode=pl.Buffered(1)),
        out_shape=jax.ShapeDtypeStruct((m, d), F32),
        compiler_params=_params(("parallel", "arbitrary")),
        name="proj_ln",
    )(h, w, x, ln_g, ln_b)


def _prepare(ffn1_w_gate, ffn1_w_up, ffn1_w_down, ln1_g, ln1_b, w_in, conv_w, conv_b, dt_bias, a_log,
             d_skip, ssm_norm_w, sink, gate_bias, w_out, ln2_g, ln2_b,
             ffn2_w_gate, ffn2_w_up, ffn2_w_down, ln3_g, ln3_b):
    row = lambda v: v.reshape(1, -1).astype(F32)
    dt0 = QKV_WIDTH + SSM_INNER + XBC_WIDTH
    dt1 = dt0 + 2 * SSM_HEADS
    z1 = QKV_WIDTH + SSM_INNER
    w_rest = jnp.concatenate(
        [w_in[:, QKV_WIDTH:z1], w_in[:, dt1:], w_in[:, z1:dt0], w_in[:, dt0:dt1],
         jnp.zeros((D_MODEL, REST_WIDTH - (REST_DT + 2 * SSM_HEADS)), w_in.dtype)], axis=1)
    slopes = 2.0 ** (-8.0 * jnp.arange(1, N_HEADS + 1, dtype=F32) / N_HEADS)
    return dict(
        ffn1=(ffn1_w_gate.astype(BF16), ffn1_w_up.astype(BF16), ffn1_w_down.astype(BF16), row(ln1_g), row(ln1_b)),
        w_qkv=w_in[:, :QKV_WIDTH].astype(BF16),
        w_rest=w_rest.astype(BF16),
        conv_w=conv_w.astype(F32), conv_b=row(conv_b),
        dt_bias=row(dt_bias), a_log=row(a_log),
        d_skip_wide=row(jnp.repeat(d_skip.astype(F32), SSM_HEAD_DIM)),
        norm_w=row(ssm_norm_w),
        head_params=jnp.stack([slopes, sink.astype(F32)]),
        gate_bias=row(gate_bias),
        w_out=w_out.astype(BF16), ln2=(row(ln2_g), row(ln2_b)),
        ffn2=(ffn2_w_gate.astype(BF16), ffn2_w_up.astype(BF16), ffn2_w_down.astype(BF16), row(ln3_g), row(ln3_b)),
    )


def _layer(x, p):
    batch, seq, d = x.shape
    x2d = x.reshape(batch * seq, d)
    x1 = _ffn_ln(x2d, *p["ffn1"])
    qkv = _matmul(x1, p["w_qkv"], BF16, name="proj_qkv")
    rest = _matmul(x1, p["w_rest"], F32, name="proj_rest")
    gated_attn = _attention(qkv, rest, p["head_params"], p["gate_bias"], batch, seq)
    xa = _conv_silu(rest, p["conv_w"], p["conv_b"], batch, seq)
    dtt, cumt = _dt_prep(rest, p["dt_bias"], p["a_log"])
    y_bwd = _ssd(xa, dtt, cumt, batch, seq, reverse=True)
    h = _ssd(xa, dtt, cumt, batch, seq, reverse=False,
             finish_args=(y_bwd, rest, gated_attn, p["d_skip_wide"], p["norm_w"], p["gate_bias"]))
    x2 = _proj_ln(h, p["w_out"], x1, *p["ln2"])
    y = _ffn_ln(x2, *p["ffn2"])
    return y.reshape(batch, seq, d)


def kernel(x_prompt, x_sample, ffn1_w_gate, ffn1_w_up, ffn1_w_down, ln1_g, ln1_b, w_in, conv_w, conv_b, dt_bias, a_log, d_skip, ssm_norm_w, sink, gate_bias, w_out, ln2_g, ln2_b, ffn2_w_gate, ffn2_w_up, ffn2_w_down, ln3_g, ln3_b):
    weights = (ffn1_w_gate, ffn1_w_up, ffn1_w_down, ln1_g, ln1_b, w_in, conv_w, conv_b, dt_bias, a_log,
               d_skip, ssm_norm_w, sink, gate_bias, w_out, ln2_g, ln2_b,
               ffn2_w_gate, ffn2_w_up, ffn2_w_down, ln3_g, ln3_b)
    p = _prepare(*[w[0] for w in weights])
    return (_layer(x_prompt, p), _layer(x_sample, p))
```

```python
import functools

import jax
import jax.numpy as jnp
from jax import lax
from jax.experimental import pallas as pl
from jax.experimental.pallas import tpu as pltpu

F32 = jnp.float32
BF16 = jnp.bfloat16

D_MODEL = 4096
DEPTH = 1
N_HEADS = 32
N_KV_HEADS = 8
HEAD_DIM = 128
KV_REP = N_HEADS // N_KV_HEADS
ATTN_WIDTH = N_HEADS * HEAD_DIM
KV_WIDTH = N_KV_HEADS * HEAD_DIM
WINDOW = 128
BLOCK = 128
SSM_HEAD_DIM = 64
SSM_HEADS = 64
SSM_INNER = 4096
SSM_GROUPS = 8
SSM_STATE = 128
HEADS_PER_GROUP = SSM_HEADS // SSM_GROUPS
GROUP_WIDTH = HEADS_PER_GROUP * SSM_HEAD_DIM
XBC_WIDTH = SSM_INNER + 2 * SSM_GROUPS * SSM_STATE
CONV_WIDTH = 5
CONV_HALF = CONV_WIDTH // 2
CHUNK = 128
D_FF = 11008
ALPHA = (2 * DEPTH) ** 0.25
LN_EPS = 1e-5
RMS_EPS = 1e-5
LOG2E = 1.4426950408889634

SUBLANES = 8
BF16_SUBLANES = 16
VMEM_LIMIT = 56 * 1024 * 1024

REST_Z = 0
REST_GATE = SSM_INNER
REST_XBC = REST_GATE + 2 * D_MODEL
REST_DT = REST_XBC + XBC_WIDTH
REST_TILE = 512
REST_WIDTH = ((REST_DT + 2 * SSM_HEADS + REST_TILE - 1) // REST_TILE) * REST_TILE
QKV_WIDTH = ATTN_WIDTH + 2 * KV_WIDTH


def _params(semantics):
    return pltpu.CompilerParams(dimension_semantics=semantics, vmem_limit_bytes=VMEM_LIMIT)


def _layer_norm_rows(y, g, b):
    mu = jnp.mean(y, axis=-1, keepdims=True)
    yc = y - mu
    var = jnp.mean(yc * yc, axis=-1, keepdims=True)
    return yc * lax.rsqrt(var + LN_EPS) * g + b


def _residual_ln_inplace(x_ref, o_ref, g_ref, b_ref, branch_scale, rows=64):
    g = g_ref[...]
    b = b_ref[...]

    def body(r, carry):
        sl = pl.ds(pl.multiple_of(r * rows, rows), rows)
        y = ALPHA * x_ref[sl, :] + branch_scale * o_ref[sl, :]
        o_ref[sl, :] = _layer_norm_rows(y, g, b)
        return carry

    lax.fori_loop(0, o_ref.shape[0] // rows, body, 0)


ROW_TILE = 1024


FFN_TILE = 256
FFN_ROWS = 64


def _ffn_ln_kernel(x_ref, wg_hbm, wu_hbm, wd_hbm, g_ref, b_ref, o_ref, xb_ref, wg_buf, wu_buf, wd_buf, sem):
    tm = o_ref.shape[0]
    tf = wd_buf.shape[1]
    n_tiles = wd_hbm.shape[0] // tf

    def weight_copies(j, slot):
        cols = pl.ds(pl.multiple_of(j * tf, tf), tf)
        return (pltpu.make_async_copy(wg_hbm.at[:, cols], wg_buf.at[slot], sem.at[0, slot]),
                pltpu.make_async_copy(wu_hbm.at[:, cols], wu_buf.at[slot], sem.at[1, slot]),
                pltpu.make_async_copy(wd_hbm.at[cols, :], wd_buf.at[slot], sem.at[2, slot]))

    for cp in weight_copies(0, 0):
        cp.start()

    def prep(r, carry):
        sl = pl.ds(pl.multiple_of(r * FFN_ROWS, FFN_ROWS), FFN_ROWS)
        xb_ref[sl, :] = x_ref[sl, :].astype(BF16)
        o_ref[sl, :] = jnp.zeros((FFN_ROWS, o_ref.shape[1]), F32)
        return carry

    lax.fori_loop(0, tm // FFN_ROWS, prep, 0)

    def tile(j, carry):
        slot = j % 2

        @pl.when(j + 1 < n_tiles)
        def _():
            for cp in weight_copies(j + 1, 1 - slot):
                cp.start()

        for cp in weight_copies(j, slot):
            cp.wait()
        xb = xb_ref[...]
        gate = jnp.dot(xb, wg_buf[slot], preferred_element_type=F32)
        up = jnp.dot(xb, wu_buf[slot], preferred_element_type=F32)
        h = (gate * jax.nn.sigmoid(gate) * up).astype(BF16)
        o_ref[...] += jnp.dot(h, wd_buf[slot], preferred_element_type=F32)
        return carry

    lax.fori_loop(0, n_tiles, tile, 0)
    _residual_ln_inplace(x_ref, o_ref, g_ref, b_ref, 0.5)


def _ffn_ln(x, wg, wu, wd, ln_g, ln_b, tm=ROW_TILE, tf=FFN_TILE):
    m, d = x.shape
    tm = min(tm, m)
    return pl.pallas_call(
        _ffn_ln_kernel,
        grid=(m // tm,),
        in_specs=[
            pl.BlockSpec((tm, d), lambda i: (i, 0), pipeline_mode=pl.Buffered(1)),
            pl.BlockSpec(memory_space=pl.ANY),
            pl.BlockSpec(memory_space=pl.ANY),
            pl.BlockSpec(memory_space=pl.ANY),
            pl.BlockSpec((1, d), lambda i: (0, 0)),
            pl.BlockSpec((1, d), lambda i: (0, 0)),
        ],
        out_specs=pl.BlockSpec((tm, d), lambda i: (i, 0), pipeline_mode=pl.Buffered(1)),
        out_shape=jax.ShapeDtypeStruct((m, d), F32),
        scratch_shapes=[
            pltpu.VMEM((tm, d), BF16),
            pltpu.VMEM((2, d, tf), BF16),
            pltpu.VMEM((2, d, tf), BF16),
            pltpu.VMEM((2, tf, d), BF16),
            pltpu.SemaphoreType.DMA((3, 2)),
        ],
        compiler_params=_params(("parallel",)),
        name="ffn_ln",
    )(x, wg, wu, wd, ln_g, ln_b)


def _matmul_kernel(x_ref, w_ref, o_ref, xb_ref):
    @pl.when(pl.program_id(1) == 0)
    def _():
        xb_ref[...] = x_ref[...].astype(BF16)

    o_ref[...] = jnp.dot(xb_ref[...], w_ref[...], preferred_element_type=F32).astype(o_ref.dtype)


def _matmul(x, w, out_dtype, tm=ROW_TILE, tn=512, name="matmul"):
    m, k = x.shape
    tm = min(tm, m)
    n = w.shape[1]
    return pl.pallas_call(
        _matmul_kernel,
        grid=(m // tm, n // tn),
        in_specs=[
            pl.BlockSpec((tm, k), lambda i, j: (i, 0)),
            pl.BlockSpec((k, tn), lambda i, j: (0, j)),
        ],
        out_specs=pl.BlockSpec((tm, tn), lambda i, j: (i, j)),
        out_shape=jax.ShapeDtypeStruct((m, n), out_dtype),
        scratch_shapes=[pltpu.VMEM((tm, k), BF16)],
        compiler_params=_params(("parallel", "arbitrary")),
        name=name,
    )(x, w)


def _attn_kernel(hp_ref, q_ref, kp_ref, kc_ref, kn_ref, vp_ref, vc_ref, vn_ref, ga_ref, ba_ref, o_ref):
    i = pl.program_id(1)
    nb = pl.num_programs(1)
    kv_heads = kc_ref.shape[1] // HEAD_DIM

    t_idx = lax.broadcasted_iota(jnp.int32, (BLOCK, 3 * BLOCK), 0)
    s_idx = lax.broadcasted_iota(jnp.int32, (BLOCK, 3 * BLOCK), 1)
    dist = jnp.abs(t_idx - s_idx + BLOCK)
    in_seq = ((s_idx >= BLOCK) | (i > 0)) & ((s_idx < 2 * BLOCK) | (i < nb - 1))
    valid = (dist <= WINDOW) & in_seq
    dist_f = dist.astype(F32)

    for kv in range(kv_heads):
        kcols = slice(kv * HEAD_DIM, (kv + 1) * HEAD_DIM)
        q0 = kv * KV_REP * HEAD_DIM
        qs = jnp.concatenate([q_ref[:, q0 + r * HEAD_DIM:q0 + (r + 1) * HEAD_DIM] for r in range(KV_REP)], axis=0)
        k = jnp.concatenate([kp_ref[:, kcols], kc_ref[:, kcols], kn_ref[:, kcols]], axis=0)
        v = jnp.concatenate([vp_ref[:, kcols], vc_ref[:, kcols], vn_ref[:, kcols]], axis=0)
        s = lax.dot_general(qs, k, (((1,), (1,)), ((), ())), preferred_element_type=F32)
        s = s * (HEAD_DIM ** -0.5 * LOG2E)
        probs = []
        for r in range(KV_REP):
            head = (pl.program_id(2) * kv_heads + kv) * KV_REP + r
            slope = hp_ref[0, head] * LOG2E
            sink = hp_ref[1, head] * LOG2E
            sc = s[r * BLOCK:(r + 1) * BLOCK] - slope * dist_f
            sc = jnp.where(valid, sc, -1e30)
            m = jnp.maximum(jnp.max(sc, axis=-1, keepdims=True), sink)
            p = jnp.exp2(sc - m)
            denom = jnp.sum(p, axis=-1, keepdims=True) + jnp.exp2(sink - m)
            probs.append((p * (1.0 / denom)).astype(BF16))
        pn = jnp.concatenate(probs, axis=0)
        out = jnp.dot(pn, v, preferred_element_type=F32)
        for r in range(KV_REP):
            cols = slice(q0 + r * HEAD_DIM, q0 + (r + 1) * HEAD_DIM)
            gate = jax.nn.sigmoid(ga_ref[:, cols] + ba_ref[:, cols])
            o_ref[:, cols] = gate * out[r * BLOCK:(r + 1) * BLOCK]


ATTN_KV_PER_STEP = 8


def _attention(qkv, rest, head_params, gate_bias, batch, seq):
    nb = seq // BLOCK
    qkv3 = qkv.reshape(batch, seq, QKV_WIDTH)
    rest3 = rest.reshape(batch, seq, REST_WIDTH)
    kvw = ATTN_KV_PER_STEP * HEAD_DIM
    k_col = ATTN_WIDTH // kvw
    v_col = (ATTN_WIDTH + KV_WIDTH) // kvw
    qw = ATTN_KV_PER_STEP * KV_REP * HEAD_DIM
    assert REST_GATE % qw == 0 and ATTN_WIDTH % kvw == 0 and KV_WIDTH % kvw == 0
    ga_col = REST_GATE // qw

    def band(col0, shift):
        def index(b, i, g):
            return (b, jnp.clip(i + shift, 0, nb - 1), col0 + g)
        return pl.BlockSpec((None, BLOCK, kvw), index)

    out = pl.pallas_call(
        _attn_kernel,
        grid=(batch, nb, N_KV_HEADS // ATTN_KV_PER_STEP),
        in_specs=[
            pl.BlockSpec(memory_space=pltpu.SMEM),
            pl.BlockSpec((None, BLOCK, qw), lambda b, i, g: (b, i, g)),
            band(k_col, -1), band(k_col, 0), band(k_col, 1),
            band(v_col, -1), band(v_col, 0), band(v_col, 1),
            pl.BlockSpec((None, BLOCK, qw), lambda b, i, g: (b, i, ga_col + g)),
            pl.BlockSpec((1, qw), lambda b, i, g: (0, g)),
        ],
        out_specs=pl.BlockSpec((None, BLOCK, qw), lambda b, i, g: (b, i, g)),
        out_shape=jax.ShapeDtypeStruct((batch, seq, ATTN_WIDTH), F32),
        compiler_params=_params(("parallel", "parallel", "parallel")),
        name="attention",
    )(head_params, qkv3, qkv3, qkv3, qkv3, qkv3, qkv3, qkv3, rest3, gate_bias)
    return out.reshape(batch * seq, ATTN_WIDTH)


def _conv_kernel(prev_ref, cur_ref, next_ref, w_ref, b_ref, o_ref, ext_ref):
    i = pl.program_id(1)
    tb = cur_ref.shape[0]
    zeros = jnp.zeros(prev_ref.shape, F32)
    ext_ref[0:SUBLANES, :] = jnp.where(i > 0, prev_ref[...], zeros)
    ext_ref[SUBLANES:SUBLANES + tb, :] = cur_ref[...]
    ext_ref[SUBLANES + tb:2 * SUBLANES + tb, :] = jnp.where(i < pl.num_programs(1) - 1, next_ref[...], zeros)
    ext = ext_ref[...]
    rows = ext.shape[0]
    acc = jnp.zeros(cur_ref.shape, F32)
    for k in range(CONV_WIDTH):
        rolled = ext if k == CONV_HALF else pltpu.roll(ext, (CONV_HALF - k) % rows, 0)
        acc = acc + w_ref[k:k + 1, :] * rolled[SUBLANES:SUBLANES + tb]
    acc = acc + b_ref[...]
    o_ref[...] = acc * jax.nn.sigmoid(acc)


def _conv_silu(rest, conv_w, conv_b, batch, seq, tb=512, tc=512):
    tb = min(tb, seq)
    rest3 = rest.reshape(batch, seq, REST_WIDTH)
    nt = seq // tb
    col0 = REST_XBC // tc
    sub_per_blk = tb // SUBLANES
    last_sub = seq // SUBLANES - 1
    out = pl.pallas_call(
        _conv_kernel,
        grid=(batch, nt, XBC_WIDTH // tc),
        in_specs=[
            pl.BlockSpec((None, SUBLANES, tc),
                         lambda b, i, c: (b, jnp.maximum(i * sub_per_blk - 1, 0), col0 + c)),
            pl.BlockSpec((None, tb, tc), lambda b, i, c: (b, i, col0 + c)),
            pl.BlockSpec((None, SUBLANES, tc),
                         lambda b, i, c: (b, jnp.minimum((i + 1) * sub_per_blk, last_sub), col0 + c)),
            pl.BlockSpec((CONV_WIDTH, tc), lambda b, i, c: (0, c)),
            pl.BlockSpec((1, tc), lambda b, i, c: (0, c)),
        ],
        out_specs=pl.BlockSpec((None, tb, tc), lambda b, i, c: (b, i, c)),
        out_shape=jax.ShapeDtypeStruct((batch, seq, XBC_WIDTH), F32),
        scratch_shapes=[pltpu.VMEM((tb + 2 * SUBLANES, tc), F32)],
        compiler_params=_params(("parallel", "parallel", "parallel")),
        name="conv_silu",
    )(rest3, rest3, rest3, conv_w, conv_b)
    return out.reshape(batch * seq, XBC_WIDTH)


def _split3(x):
    hi = x.astype(BF16)
    r1 = x - hi.astype(F32)
    mid = r1.astype(BF16)
    lo = (r1 - mid.astype(F32)).astype(BF16)
    return hi, mid, lo


def _dt_kernel(raw_ref, bias_ref, alog_ref, dtt_ref, cumt_ref):
    x = raw_ref[...] + bias_ref[...]
    dt = jnp.maximum(x, 0.0) + jnp.log1p(jnp.exp(-jnp.abs(x)))
    da = dt * (-jnp.exp(alog_ref[...]))
    t_idx = lax.broadcasted_iota(jnp.int32, (CHUNK, CHUNK), 0)
    s_idx = lax.broadcasted_iota(jnp.int32, (CHUNK, CHUNK), 1)
    lower = (s_idx <= t_idx).astype(BF16)
    upper = (s_idx >= t_idx).astype(BF16)
    cum_f = jnp.zeros(da.shape, F32)
    cum_r = jnp.zeros(da.shape, F32)
    for piece in _split3(da):
        cum_f = cum_f + jnp.dot(lower, piece, preferred_element_type=F32)
        cum_r = cum_r + jnp.dot(upper, piece, preferred_element_type=F32)
    col = lax.broadcasted_iota(jnp.int32, da.shape, 1)
    cum = jnp.where(col < SSM_HEADS, cum_f, cum_r)
    dtt_ref[...] = dt.T
    cumt_ref[...] = cum.T


def _dt_prep(rest, dt_bias, a_log):
    m = rest.shape[0]
    nchunks = m // CHUNK
    w = 2 * SSM_HEADS
    return pl.pallas_call(
        _dt_kernel,
        grid=(nchunks,),
        in_specs=[
            pl.BlockSpec((CHUNK, w), lambda c: (c, REST_DT // w)),
            pl.BlockSpec((1, w), lambda c: (0, 0)),
            pl.BlockSpec((1, w), lambda c: (0, 0)),
        ],
        out_specs=[
            pl.BlockSpec((w, CHUNK), lambda c: (c, 0)),
            pl.BlockSpec((w, CHUNK), lambda c: (c, 0)),
        ],
        out_shape=[
            jax.ShapeDtypeStruct((nchunks * w, CHUNK), F32),
            jax.ShapeDtypeStruct((nchunks * w, CHUNK), F32),
        ],
        compiler_params=_params(("parallel",)),
        name="dt_prep",
    )(rest, dt_bias, a_log)


def _ssd_group(xs, bm, cm, dtt, cumt, state, reverse):
    hg = HEADS_PER_GROUP
    p = SSM_HEAD_DIM
    tot = cumt[:, 0:1] if reverse else cumt[:, CHUNK - 1:CHUNK]

    narrow = jnp.concatenate([dtt * jnp.exp(tot - cumt), jnp.exp(cumt)], axis=1)
    narrow = jnp.concatenate([narrow, jnp.zeros((BF16_SUBLANES - hg, 2 * CHUNK), F32)], axis=0)
    pieces = jnp.concatenate(_split3(narrow), axis=0)
    head_of_lane = lax.broadcasted_iota(jnp.int32, (3 * BF16_SUBLANES, GROUP_WIDTH), 1) // p
    piece_row = lax.broadcasted_iota(jnp.int32, (3 * BF16_SUBLANES, GROUP_WIDTH), 0) % BF16_SUBLANES
    expand = (head_of_lane == piece_row).astype(BF16)
    wide = lax.dot_general(pieces, expand, (((0,), (0,)), ((), ())), preferred_element_type=F32)
    todecay_w = wide[0:CHUNK]
    ecum_w = wide[CHUNK:2 * CHUNK]
    etot_w = ecum_w[0:1] if reverse else ecum_w[CHUNK - 1:CHUNK]
    xb = xs.astype(BF16)
    xend = (xs * todecay_w).astype(BF16)

    cb = lax.dot_general(cm, bm, (((1,), (1,)), ((), ())), preferred_element_type=F32)
    y_off =jnp.dot(cm, state.astype(BF16), preferred_element_type=F32) * ecum_w

    cum2 = cumt * LOG2E
    cum2_col = jnp.concatenate([cum2, jnp.zeros((CHUNK - hg, CHUNK), F32)], axis=0).T
    t_idx = lax.broadcasted_iota(jnp.int32, (CHUNK, CHUNK), 0)
    s_idx = lax.broadcasted_iota(jnp.int32, (CHUNK, CHUNK), 1)
    mask = (s_idx >= t_idx) if reverse else (s_idx <= t_idx)
    lane = lax.broadcasted_iota(jnp.int32, (CHUNK, 2 * p), 1)
    y_diag = []
    for pair in range(hg // 2):
        mats = []
        for h in (2 * pair, 2 * pair + 1):
            seg2 = cum2_col[:, h:h + 1] - cum2[h:h + 1, :]
            decay = jnp.exp2(jnp.where(mask, seg2, -jnp.inf))
            mats.append((cb * decay * dtt[h:h + 1, :]).astype(BF16))
        lhs = jnp.concatenate(mats, axis=1)
        xp = xb[:, pair * 2 * p:(pair + 1) * 2 * p]
        zero = jnp.zeros_like(xp)
        rhs = jnp.concatenate([jnp.where(lane < p, xp, zero), jnp.where(lane >= p, xp, zero)], axis=0)
        y_diag.append(jnp.dot(lhs, rhs, preferred_element_type=F32))
    y = y_off + jnp.concatenate(y_diag, axis=1)

    new_state = lax.dot_general(bm, xend, (((0,), (0,)), ((), ())), preferred_element_type=F32)
    return y, state * etot_w + new_state


def _ssd_kernel(xs_ref, b_ref, c_ref, dtt_ref, cumt_ref, *rest, reverse, finish):
    if finish:
        yb_ref, z_ref, gb_ref, att_ref, dsk_ref, nw_ref, bb_ref, o_ref, state_ref = rest
    else:
        o_ref, state_ref = rest
    hg = HEADS_PER_GROUP

    @pl.when(pl.program_id(2) == 0)
    def _():
        state_ref[...] = jnp.zeros_like(state_ref)

    for gi in range(state_ref.shape[0]):
        cols = slice(gi * GROUP_WIDTH, (gi + 1) * GROUP_WIDTH)
        ncols = slice(gi * SSM_STATE, (gi + 1) * SSM_STATE)
        heads = slice(gi * hg, (gi + 1) * hg)
        xs = xs_ref[:, cols]
        y, state = _ssd_group(xs, b_ref[:, ncols].astype(BF16), c_ref[:, ncols].astype(BF16),
                              dtt_ref[heads, :], cumt_ref[heads, :], state_ref[gi], reverse)
        state_ref[gi] = state
        if finish:
            z = z_ref[:, cols]
            y = (y + yb_ref[:, cols] + dsk_ref[:, cols] * xs) * (z * jax.nn.sigmoid(z))
            ms = jnp.mean(y * y, axis=-1, keepdims=True)
            ssm = y * lax.rsqrt(ms + RMS_EPS) * nw_ref[:, cols]
            gate = jax.nn.sigmoid(gb_ref[:, cols] + bb_ref[:, cols])
            o_ref[:, cols] = (att_ref[:, cols] + gate * ssm).astype(o_ref.dtype)
        else:
            o_ref[:, cols] = y


SSD_GROUPS_PER_STEP = 8


def _ssd(xa, dtt, cumt, batch, seq, reverse, finish_args=None):
    m = batch * seq
    nc = seq // CHUNK
    gps = SSD_GROUPS_PER_STEP
    gw = gps * GROUP_WIDTH
    nw = gps * SSM_STATE
    assert SSM_INNER % nw == 0 and (SSM_GROUPS * SSM_STATE) % nw == 0
    assert REST_Z % gw == 0 and (REST_GATE + D_MODEL) % gw == 0 and D_MODEL % gw == 0
    b_col = SSM_INNER // nw
    c_col = b_col + SSM_GROUPS // gps
    blocks_per_chunk = 2 * SSM_GROUPS // gps
    direction = 1 if reverse else 0

    def row(b, c):
        return b * nc + ((nc - 1 - c) if reverse else c)

    def blk(width, col0):
        return pl.BlockSpec((CHUNK, width), lambda b, g, c: (row(b, c), col0 + g))

    def small():
        return pl.BlockSpec((gps * HEADS_PER_GROUP, CHUNK),
                            lambda b, g, c: (row(b, c) * blocks_per_chunk + direction * (SSM_GROUPS // gps) + g, 0))

    def vec(col0):
        return pl.BlockSpec((1, gw), lambda b, g, c: (0, col0 + g))

    in_specs = [blk(gw, 0), blk(nw, b_col), blk(nw, c_col), small(), small()]
    args = [xa, xa, xa, dtt, cumt]
    finish = finish_args is not None
    if finish:
        y_bwd, rest, gated_attn, d_skip_wide, norm_w, gate_bias = finish_args
        in_specs += [blk(gw, 0), blk(gw, REST_Z // gw), blk(gw, (REST_GATE + D_MODEL) // gw), blk(gw, 0),
                     vec(0), vec(0), vec(D_MODEL // gw)]
        args += [y_bwd, rest, rest, gated_attn, d_skip_wide, norm_w, gate_bias]
    return pl.pallas_call(
        functools.partial(_ssd_kernel, reverse=reverse, finish=finish),
        grid=(batch, SSM_GROUPS // gps, nc),
        in_specs=in_specs,
        out_specs=blk(gw, 0),
        out_shape=jax.ShapeDtypeStruct((m, SSM_INNER), BF16 if finish else F32),
        scratch_shapes=[pltpu.VMEM((gps, SSM_STATE, GROUP_WIDTH), F32)],
        compiler_params=_params(("parallel", "parallel", "arbitrary")),
        name="ssd_fwd_merge" if finish else "ssd_bwd",
    )(*args)


def _proj_ln_kernel(h_ref, w_ref, x_ref, g_ref, b_ref, o_ref):
    j = pl.program_id(1)

    @pl.when(j == 0)
    def _():
        o_ref[...] = jnp.zeros_like(o_ref)

    o_ref[...] += jnp.dot(h_ref[...], w_ref[...], preferred_element_type=F32)

    @pl.when(j == pl.num_programs(1) - 1)
    def _():
        _residual_ln_inplace(x_ref, o_ref, g_ref, b_ref, 1.0)


def _proj_ln(h, w, x, ln_g, ln_b, tm=ROW_TILE, tk=1024):
    m, k = h.shape
    tm = min(tm, m)
    d = w.shape[1]
    return pl.pallas_call(
        _proj_ln_kernel,
        grid=(m // tm, k // tk),
        in_specs=[
            pl.BlockSpec((tm, tk), lambda i, j: (i, j)),
            pl.BlockSpec((tk, d), lambda i, j: (j, 0)),
            pl.BlockSpec((tm, d), lambda i, j: (i, 0), pipeline_mode=pl.Buffered(1)),
            pl.BlockSpec((1, d), lambda i, j: (0, 0)),
            pl.BlockSpec((1, d), lambda i, j: (0, 0)),
        ],
        out_specs=pl.BlockSpec((tm, d), lambda i, j: (i, 0), pipeline_mode=pl.Buffered(1)),
        out_shape=jax.ShapeDtypeStruct((m, d), F32),
        compiler_params=_params(("parallel", "arbitrary")),
        name="proj_ln",
    )(h, w, x, ln_g, ln_b)


def _prepare(ffn1_w_gate, ffn1_w_up, ffn1_w_down, ln1_g, ln1_b, w_in, conv_w, conv_b, dt_bias, a_log,
             d_skip, ssm_norm_w, sink, gate_bias, w_out, ln2_g, ln2_b,
             ffn2_w_gate, ffn2_w_up, ffn2_w_down, ln3_g, ln3_b):
    row = lambda v: v.reshape(1, -1).astype(F32)
    dt0 = QKV_WIDTH + SSM_INNER + XBC_WIDTH
    dt1 = dt0 + 2 * SSM_HEADS
    z1 = QKV_WIDTH + SSM_INNER
    w_rest = jnp.concatenate(
        [w_in[:, QKV_WIDTH:z1], w_in[:, dt1:], w_in[:, z1:dt0], w_in[:, dt0:dt1],
         jnp.zeros((D_MODEL, REST_WIDTH - (REST_DT + 2 * SSM_HEADS)), w_in.dtype)], axis=1)
    slopes = 2.0 ** (-8.0 * jnp.arange(1, N_HEADS + 1, dtype=F32) / N_HEADS)
    return dict(
        ffn1=(ffn1_w_gate.astype(BF16), ffn1_w_up.astype(BF16), ffn1_w_down.astype(BF16), row(ln1_g), row(ln1_b)),
        w_qkv=w_in[:, :QKV_WIDTH].astype(BF16),
        w_rest=w_rest.astype(BF16),
        conv_w=conv_w.astype(F32), conv_b=row(conv_b),
        dt_bias=row(dt_bias), a_log=row(a_log),
        d_skip_wide=row(jnp.repeat(d_skip.astype(F32), SSM_HEAD_DIM)),
        norm_w=row(ssm_norm_w),
        head_params=jnp.stack([slopes, sink.astype(F32)]),
        gate_bias=row(gate_bias),
        w_out=w_out.astype(BF16), ln2=(row(ln2_g), row(ln2_b)),
        ffn2=(ffn2_w_gate.astype(BF16), ffn2_w_up.astype(BF16), ffn2_w_down.astype(BF16), row(ln3_g), row(ln3_b)),
    )


def _layer(x, p):
    batch, seq, d = x.shape
    x2d = x.reshape(batch * seq, d)
    x1 = _ffn_ln(x2d, *p["ffn1"])
    qkv = _matmul(x1, p["w_qkv"], BF16, name="proj_qkv")
    rest = _matmul(x1, p["w_rest"], F32, name="proj_rest")
    gated_attn = _attention(qkv, rest, p["head_params"], p["gate_bias"], batch, seq)
    xa = _conv_silu(rest, p["conv_w"], p["conv_b"], batch, seq)
    dtt, cumt = _dt_prep(rest, p["dt_bias"], p["a_log"])
    y_bwd = _ssd(xa, dtt, cumt, batch, seq, reverse=True)
    h = _ssd(xa, dtt, cumt, batch, seq, reverse=False,
             finish_args=(y_bwd, rest, gated_attn, p["d_skip_wide"], p["norm_w"], p["gate_bias"]))
    x2 = _proj_ln(h, p["w_out"], x1, *p["ln2"])
    y = _ffn_ln(x2, *p["ffn2"])
    return y.reshape(batch, seq, d)


def kernel(x_prompt, x_sample, ffn1_w_gate, ffn1_w_up, ffn1_w_down, ln1_g, ln1_b, w_in, conv_w, conv_b, dt_bias, a_log, d_skip, ssm_norm_w, sink, gate_bias, w_out, ln2_g, ln2_b, ffn2_w_gate, ffn2_w_up, ffn2_w_down, ln3_g, ln3_b):
    weights = (ffn1_w_gate, ffn1_w_up, ffn1_w_down, ln1_g, ln1_b, w_in, conv_w, conv_b, dt_bias, a_log,
               d_skip, ssm_norm_w, sink, gate_bias, w_out, ln2_g, ln2_b,
               ffn2_w_gate, ffn2_w_up, ffn2_w_down, ln3_g, ln3_b)
    p = _prepare(*[w[0] for w in weights])
    return (_layer(x_prompt, p), _layer(x_sample, p))
```

```python
import functools

import jax
import jax.numpy as jnp
from jax import lax
from jax.experimental import pallas as pl
from jax.experimental.pallas import tpu as pltpu

F32 = jnp.float32
BF16 = jnp.bfloat16

D_MODEL = 4096
DEPTH = 1
N_HEADS = 32
N_KV_HEADS = 8
HEAD_DIM = 128
KV_REP = N_HEADS // N_KV_HEADS
ATTN_WIDTH = N_HEADS * HEAD_DIM
KV_WIDTH = N_KV_HEADS * HEAD_DIM
WINDOW = 128
BLOCK = 128
SSM_HEAD_DIM = 64
SSM_HEADS = 64
SSM_INNER = 4096
SSM_GROUPS = 8
SSM_STATE = 128
HEADS_PER_GROUP = SSM_HEADS // SSM_GROUPS
GROUP_WIDTH = HEADS_PER_GROUP * SSM_HEAD_DIM
XBC_WIDTH = SSM_INNER + 2 * SSM_GROUPS * SSM_STATE
CONV_WIDTH = 5
CONV_HALF = CONV_WIDTH // 2
CHUNK = 128
D_FF = 11008
ALPHA = (2 * DEPTH) ** 0.25
LN_EPS = 1e-5
RMS_EPS = 1e-5
LOG2E = 1.4426950408889634

SUBLANES = 8
BF16_SUBLANES = 16
VMEM_LIMIT = 56 * 1024 * 1024

REST_Z = 0
REST_GATE = SSM_INNER
REST_XBC = REST_GATE + 2 * D_MODEL
REST_DT = REST_XBC + XBC_WIDTH
REST_TILE = 512
REST_WIDTH = ((REST_DT + 2 * SSM_HEADS + REST_TILE - 1) // REST_TILE) * REST_TILE
QKV_WIDTH = ATTN_WIDTH + 2 * KV_WIDTH


def _params(semantics):
    return pltpu.CompilerParams(dimension_semantics=semantics, vmem_limit_bytes=VMEM_LIMIT)


def _layer_norm_rows(y, g, b):
    mu = jnp.mean(y, axis=-1, keepdims=True)
    yc = y - mu
    var = jnp.mean(yc * yc, axis=-1, keepdims=True)
    return yc * lax.rsqrt(var + LN_EPS) * g + b


def _residual_ln_inplace(x_ref, o_ref, g_ref, b_ref, branch_scale, rows=64):
    g = g_ref[...]
    b = b_ref[...]

    def body(r, carry):
        sl = pl.ds(pl.multiple_of(r * rows, rows), rows)
        y = ALPHA * x_ref[sl, :] + branch_scale * o_ref[sl, :]
        o_ref[sl, :] = _layer_norm_rows(y, g, b)
        return carry

    lax.fori_loop(0, o_ref.shape[0] // rows, body, 0)


ROW_TILE = 1024


FFN_TILE = 256
FFN_ROWS = 64


def _ffn_ln_kernel(x_ref, wg_hbm, wu_hbm, wd_hbm, g_ref, b_ref, o_ref, xb_ref, wg_buf, wu_buf, wd_buf, sem, *,
                   emit_bf16):
    tm = o_ref.shape[0]
    tf = wd_buf.shape[1]
    n_tiles = wd_hbm.shape[0] // tf

    def weight_copies(j, slot):
        cols = pl.ds(pl.multiple_of(j * tf, tf), tf)
        return (pltpu.make_async_copy(wg_hbm.at[:, cols], wg_buf.at[slot], sem.at[0, slot]),
                pltpu.make_async_copy(wu_hbm.at[:, cols], wu_buf.at[slot], sem.at[1, slot]),
                pltpu.make_async_copy(wd_hbm.at[cols, :], wd_buf.at[slot], sem.at[2, slot]))

    for cp in weight_copies(0, 0):
        cp.start()

    def prep(r, carry):
        sl = pl.ds(pl.multiple_of(r * FFN_ROWS, FFN_ROWS), FFN_ROWS)
        xb_ref[sl, :] = x_ref[sl, :].astype(BF16)
        o_ref[sl, :] = jnp.zeros((FFN_ROWS, o_ref.shape[1]), F32)
        return carry

    lax.fori_loop(0, tm // FFN_ROWS, prep, 0)

    def tile(j, carry):
        slot = j % 2

        @pl.when(j + 1 < n_tiles)
        def _():
            for cp in weight_copies(j + 1, 1 - slot):
                cp.start()

        for cp in weight_copies(j, slot):
            cp.wait()
        xb = xb_ref[...]
        gate = jnp.dot(xb, wg_buf[slot], preferred_element_type=F32)
        up = jnp.dot(xb, wu_buf[slot], preferred_element_type=F32)
        h = (gate * jax.nn.sigmoid(gate) * up).astype(BF16)
        o_ref[...] += jnp.dot(h, wd_buf[slot], preferred_element_type=F32)
        return carry

    lax.fori_loop(0, n_tiles, tile, 0)
    _residual_ln_inplace(x_ref, o_ref, g_ref, b_ref, 0.5)
    if emit_bf16:
        def narrow(r, carry):
            sl = pl.ds(pl.multiple_of(r * FFN_ROWS, FFN_ROWS), FFN_ROWS)
            xb_ref[sl, :] = o_ref[sl, :].astype(BF16)
            return carry

        lax.fori_loop(0, tm // FFN_ROWS, narrow, 0)


def _ffn_ln(x, wg, wu, wd, ln_g, ln_b, emit_bf16, tm=ROW_TILE, tf=FFN_TILE):
    m, d = x.shape
    tm = min(tm, m)
    row_spec = pl.BlockSpec((tm, d), lambda i: (i, 0), pipeline_mode=pl.Buffered(1))
    xb_scratch = [] if emit_bf16 else [pltpu.VMEM((tm, d), BF16)]
    res = pl.pallas_call(
        functools.partial(_ffn_ln_kernel, emit_bf16=emit_bf16),
        grid=(m // tm,),
        in_specs=[
            row_spec,
            pl.BlockSpec(memory_space=pl.ANY),
            pl.BlockSpec(memory_space=pl.ANY),
            pl.BlockSpec(memory_space=pl.ANY),
            pl.BlockSpec((1, d), lambda i: (0, 0)),
            pl.BlockSpec((1, d), lambda i: (0, 0)),
        ],
        out_specs=[row_spec, row_spec] if emit_bf16 else [row_spec],
        out_shape=[jax.ShapeDtypeStruct((m, d), F32)] + ([jax.ShapeDtypeStruct((m, d), BF16)] if emit_bf16 else []),
        scratch_shapes=xb_scratch + [
            pltpu.VMEM((2, d, tf), BF16),
            pltpu.VMEM((2, d, tf), BF16),
            pltpu.VMEM((2, tf, d), BF16),
            pltpu.SemaphoreType.DMA((3, 2)),
        ],
        compiler_params=_params(("parallel",)),
        name="ffn_ln",
    )(x, wg, wu, wd, ln_g, ln_b)
    return res if emit_bf16 else res[0]


def _matmul_kernel(x_ref, w_ref, o_ref):
    o_ref[...] = jnp.dot(x_ref[...], w_ref[...], preferred_element_type=F32).astype(o_ref.dtype)


def _matmul(x, w, out_dtype, tm=ROW_TILE, tn=512, name="matmul"):
    m, k = x.shape
    tm = min(tm, m)
    n = w.shape[1]
    return pl.pallas_call(
        _matmul_kernel,
        grid=(m // tm, n // tn),
        in_specs=[
            pl.BlockSpec((tm, k), lambda i, j: (i, 0)),
            pl.BlockSpec((k, tn), lambda i, j: (0, j)),
        ],
        out_specs=pl.BlockSpec((tm, tn), lambda i, j: (i, j)),
        out_shape=jax.ShapeDtypeStruct((m, n), out_dtype),
        compiler_params=_params(("parallel", "arbitrary")),
        name=name,
    )(x, w)


def _attn_kernel(hp_ref, q_ref, kp_ref, kc_ref, kn_ref, vp_ref, vc_ref, vn_ref, ga_ref, ba_ref, o_ref):
    i = pl.program_id(1)
    nb = pl.num_programs(1)
    kv_heads = kc_ref.shape[1] // HEAD_DIM

    t_idx = lax.broadcasted_iota(jnp.int32, (BLOCK, 3 * BLOCK), 0)
    s_idx = lax.broadcasted_iota(jnp.int32, (BLOCK, 3 * BLOCK), 1)
    dist = jnp.abs(t_idx - s_idx + BLOCK)
    in_seq = ((s_idx >= BLOCK) | (i > 0)) & ((s_idx < 2 * BLOCK) | (i < nb - 1))
    valid = (dist <= WINDOW) & in_seq
    dist_f = dist.astype(F32)

    for kv in range(kv_heads):
        kcols = slice(kv * HEAD_DIM, (kv + 1) * HEAD_DIM)
        q0 = kv * KV_REP * HEAD_DIM
        qs = jnp.concatenate([q_ref[:, q0 + r * HEAD_DIM:q0 + (r + 1) * HEAD_DIM] for r in range(KV_REP)], axis=0)
        k = jnp.concatenate([kp_ref[:, kcols], kc_ref[:, kcols], kn_ref[:, kcols]], axis=0)
        v = jnp.concatenate([vp_ref[:, kcols], vc_ref[:, kcols], vn_ref[:, kcols]], axis=0)
        s = lax.dot_general(qs, k, (((1,), (1,)), ((), ())), preferred_element_type=F32)
        s = s * (HEAD_DIM ** -0.5 * LOG2E)
        probs = []
        for r in range(KV_REP):
            head = (pl.program_id(2) * kv_heads + kv) * KV_REP + r
            slope = hp_ref[0, head] * LOG2E
            sink = hp_ref[1, head] * LOG2E
            sc = s[r * BLOCK:(r + 1) * BLOCK] - slope * dist_f
            sc = jnp.where(valid, sc, -1e30)
            m = jnp.maximum(jnp.max(sc, axis=-1, keepdims=True), sink)
            p = jnp.exp2(sc - m)
            denom = jnp.sum(p, axis=-1, keepdims=True) + jnp.exp2(sink - m)
            probs.append((p * (1.0 / denom)).astype(BF16))
        pn = jnp.concatenate(probs, axis=0)
        out = jnp.dot(pn, v, preferred_element_type=F32)
        for r in range(KV_REP):
            cols = slice(q0 + r * HEAD_DIM, q0 + (r + 1) * HEAD_DIM)
            gate = jax.nn.sigmoid(ga_ref[:, cols] + ba_ref[:, cols])
            o_ref[:, cols] = gate * out[r * BLOCK:(r + 1) * BLOCK]


ATTN_KV_PER_STEP = 8


def _attention(qkv, rest, head_params, gate_bias, batch, seq):
    nb = seq // BLOCK
    qkv3 = qkv.reshape(batch, seq, QKV_WIDTH)
    rest3 = rest.reshape(batch, seq, REST_WIDTH)
    kvw = ATTN_KV_PER_STEP * HEAD_DIM
    k_col = ATTN_WIDTH // kvw
    v_col = (ATTN_WIDTH + KV_WIDTH) // kvw
    qw = ATTN_KV_PER_STEP * KV_REP * HEAD_DIM
    assert REST_GATE % qw == 0 and ATTN_WIDTH % kvw == 0 and KV_WIDTH % kvw == 0
    ga_col = REST_GATE // qw

    def band(col0, shift):
        def index(b, i, g):
            return (b, jnp.clip(i + shift, 0, nb - 1), col0 + g)
        return pl.BlockSpec((None, BLOCK, kvw), index)

    out = pl.pallas_call(
        _attn_kernel,
        grid=(batch, nb, N_KV_HEADS // ATTN_KV_PER_STEP),
        in_specs=[
            pl.BlockSpec(memory_space=pltpu.SMEM),
            pl.BlockSpec((None, BLOCK, qw), lambda b, i, g: (b, i, g)),
            band(k_col, -1), band(k_col, 0), band(k_col, 1),
            band(v_col, -1), band(v_col, 0), band(v_col, 1),
            pl.BlockSpec((None, BLOCK, qw), lambda b, i, g: (b, i, ga_col + g)),
            pl.BlockSpec((1, qw), lambda b, i, g: (0, g)),
        ],
        out_specs=pl.BlockSpec((None, BLOCK, qw), lambda b, i, g: (b, i, g)),
        out_shape=jax.ShapeDtypeStruct((batch, seq, ATTN_WIDTH), F32),
        compiler_params=_params(("parallel", "parallel", "parallel")),
        name="attention",
    )(head_params, qkv3, qkv3, qkv3, qkv3, qkv3, qkv3, qkv3, rest3, gate_bias)
    return out.reshape(batch * seq, ATTN_WIDTH)


def _conv_kernel(prev_ref, cur_ref, next_ref, w_ref, b_ref, o_ref, ext_ref):
    i = pl.program_id(1)
    tb = cur_ref.shape[0]
    zeros = jnp.zeros(prev_ref.shape, F32)
    ext_ref[0:SUBLANES, :] = jnp.where(i > 0, prev_ref[...], zeros)
    ext_ref[SUBLANES:SUBLANES + tb, :] = cur_ref[...]
    ext_ref[SUBLANES + tb:2 * SUBLANES + tb, :] = jnp.where(i < pl.num_programs(1) - 1, next_ref[...], zeros)
    ext = ext_ref[...]
    rows = ext.shape[0]
    acc = jnp.zeros(cur_ref.shape, F32)
    for k in range(CONV_WIDTH):
        rolled = ext if k == CONV_HALF else pltpu.roll(ext, (CONV_HALF - k) % rows, 0)
        acc = acc + w_ref[k:k + 1, :] * rolled[SUBLANES:SUBLANES + tb]
    acc = acc + b_ref[...]
    o_ref[...] = acc * jax.nn.sigmoid(acc)


def _conv_silu(rest, conv_w, conv_b, batch, seq, tb=512, tc=512):
    tb = min(tb, seq)
    rest3 = rest.reshape(batch, seq, REST_WIDTH)
    nt = seq // tb
    col0 = REST_XBC // tc
    sub_per_blk = tb // SUBLANES
    last_sub = seq // SUBLANES - 1
    out = pl.pallas_call(
        _conv_kernel,
        grid=(batch, nt, XBC_WIDTH // tc),
        in_specs=[
            pl.BlockSpec((None, SUBLANES, tc),
                         lambda b, i, c: (b, jnp.maximum(i * sub_per_blk - 1, 0), col0 + c)),
            pl.BlockSpec((None, tb, tc), lambda b, i, c: (b, i, col0 + c)),
            pl.BlockSpec((None, SUBLANES, tc),
                         lambda b, i, c: (b, jnp.minimum((i + 1) * sub_per_blk, last_sub), col0 + c)),
            pl.BlockSpec((CONV_WIDTH, tc), lambda b, i, c: (0, c)),
            pl.BlockSpec((1, tc), lambda b, i, c: (0, c)),
        ],
        out_specs=pl.BlockSpec((None, tb, tc), lambda b, i, c: (b, i, c)),
        out_shape=jax.ShapeDtypeStruct((batch, seq, XBC_WIDTH), F32),
        scratch_shapes=[pltpu.VMEM((tb + 2 * SUBLANES, tc), F32)],
        compiler_params=_params(("parallel", "parallel", "parallel")),
        name="conv_silu",
    )(rest3, rest3, rest3, conv_w, conv_b)
    return out.reshape(batch * seq, XBC_WIDTH)


def _split3(x):
    hi = x.astype(BF16)
    r1 = x - hi.astype(F32)
    mid = r1.astype(BF16)
    lo = (r1 - mid.astype(F32)).astype(BF16)
    return hi, mid, lo


def _dt_kernel(raw_ref, bias_ref, alog_ref, dtt_ref, cumt_ref):
    x = raw_ref[...] + bias_ref[...]
    dt = jnp.maximum(x, 0.0) + jnp.log1p(jnp.exp(-jnp.abs(x)))
    da = dt * (-jnp.exp(alog_ref[...]))
    t_idx = lax.broadcasted_iota(jnp.int32, (CHUNK, CHUNK), 0)
    s_idx = lax.broadcasted_iota(jnp.int32, (CHUNK, CHUNK), 1)
    lower = (s_idx <= t_idx).astype(BF16)
    upper = (s_idx >= t_idx).astype(BF16)
    cum_f = jnp.zeros(da.shape, F32)
    cum_r = jnp.zeros(da.shape, F32)
    for piece in _split3(da):
        cum_f = cum_f + jnp.dot(lower, piece, preferred_element_type=F32)
        cum_r = cum_r + jnp.dot(upper, piece, preferred_element_type=F32)
    col = lax.broadcasted_iota(jnp.int32, da.shape, 1)
    cum = jnp.where(col < SSM_HEADS, cum_f, cum_r)
    dtt_ref[...] = dt.T
    cumt_ref[...] = cum.T


def _dt_prep(rest, dt_bias, a_log):
    m = rest.shape[0]
    nchunks = m // CHUNK
    w = 2 * SSM_HEADS
    return pl.pallas_call(
        _dt_kernel,
        grid=(nchunks,),
        in_specs=[
            pl.BlockSpec((CHUNK, w), lambda c: (c, REST_DT // w)),
            pl.BlockSpec((1, w), lambda c: (0, 0)),
            pl.BlockSpec((1, w), lambda c: (0, 0)),
        ],
        out_specs=[
            pl.BlockSpec((w, CHUNK), lambda c: (c, 0)),
            pl.BlockSpec((w, CHUNK), lambda c: (c, 0)),
        ],
        out_shape=[
            jax.ShapeDtypeStruct((nchunks * w, CHUNK), F32),
            jax.ShapeDtypeStruct((nchunks * w, CHUNK), F32),
        ],
        compiler_params=_params(("parallel",)),
        name="dt_prep",
    )(rest, dt_bias, a_log)


def _ssd_group(xs, bm, cm, dtt, cumt, state, reverse):
    hg = HEADS_PER_GROUP
    p = SSM_HEAD_DIM
    tot = cumt[:, 0:1] if reverse else cumt[:, CHUNK - 1:CHUNK]

    narrow = jnp.concatenate([dtt * jnp.exp(tot - cumt), jnp.exp(cumt)], axis=1)
    narrow = jnp.concatenate([narrow, jnp.zeros((BF16_SUBLANES - hg, 2 * CHUNK), F32)], axis=0)
    pieces = jnp.concatenate(_split3(narrow), axis=0)
    head_of_lane = lax.broadcasted_iota(jnp.int32, (3 * BF16_SUBLANES, GROUP_WIDTH), 1) // p
    piece_row = lax.broadcasted_iota(jnp.int32, (3 * BF16_SUBLANES, GROUP_WIDTH), 0) % BF16_SUBLANES
    expand = (head_of_lane == piece_row).astype(BF16)
    wide = lax.dot_general(pieces, expand, (((0,), (0,)), ((), ())), preferred_element_type=F32)
    todecay_w = wide[0:CHUNK]
    ecum_w = wide[CHUNK:2 * CHUNK]
    etot_w = ecum_w[0:1] if reverse else ecum_w[CHUNK - 1:CHUNK]
    xb = xs.astype(BF16)
    xend = (xs * todecay_w).astype(BF16)

    cb = lax.dot_general(cm, bm, (((1,), (1,)), ((), ())), preferred_element_type=F32)
    y_off =jnp.dot(cm, state.astype(BF16), preferred_element_type=F32) * ecum_w

    cum2 = cumt * LOG2E
    cum2_col = jnp.concatenate([cum2, jnp.zeros((CHUNK - hg, CHUNK), F32)], axis=0).T
    t_idx = lax.broadcasted_iota(jnp.int32, (CHUNK, CHUNK), 0)
    s_idx = lax.broadcasted_iota(jnp.int32, (CHUNK, CHUNK), 1)
    mask = (s_idx >= t_idx) if reverse else (s_idx <= t_idx)
    lane = lax.broadcasted_iota(jnp.int32, (CHUNK, 2 * p), 1)
    y_diag = []
    for pair in range(hg // 2):
        mats = []
        for h in (2 * pair, 2 * pair + 1):
            seg2 = cum2_col[:, h:h + 1] - cum2[h:h + 1, :]
            decay = jnp.exp2(jnp.where(mask, seg2, -jnp.inf))
            mats.append((cb * decay * dtt[h:h + 1, :]).astype(BF16))
        lhs = jnp.concatenate(mats, axis=1)
        xp = xb[:, pair * 2 * p:(pair + 1) * 2 * p]
        zero = jnp.zeros_like(xp)
        rhs = jnp.concatenate([jnp.where(lane < p, xp, zero), jnp.where(lane >= p, xp, zero)], axis=0)
        y_diag.append(jnp.dot(lhs, rhs, preferred_element_type=F32))
    y = y_off + jnp.concatenate(y_diag, axis=1)

    new_state = lax.dot_general(bm, xend, (((0,), (0,)), ((), ())), preferred_element_type=F32)
    return y, state * etot_w + new_state


def _ssd_kernel(xs_ref, b_ref, c_ref, dtt_ref, cumt_ref, *rest, reverse, finish):
    if finish:
        yb_ref, z_ref, gb_ref, att_ref, dsk_ref, nw_ref, bb_ref, o_ref, state_ref = rest
    else:
        o_ref, state_ref = rest
    hg = HEADS_PER_GROUP

    @pl.when(pl.program_id(2) == 0)
    def _():
        state_ref[...] = jnp.zeros_like(state_ref)

    for gi in range(state_ref.shape[0]):
        cols = slice(gi * GROUP_WIDTH, (gi + 1) * GROUP_WIDTH)
        ncols = slice(gi * SSM_STATE, (gi + 1) * SSM_STATE)
        heads = slice(gi * hg, (gi + 1) * hg)
        xs = xs_ref[:, cols]
        y, state = _ssd_group(xs, b_ref[:, ncols].astype(BF16), c_ref[:, ncols].astype(BF16),
                              dtt_ref[heads, :], cumt_ref[heads, :], state_ref[gi], reverse)
        state_ref[gi] = state
        if finish:
            z = z_ref[:, cols]
            y = (y + yb_ref[:, cols] + dsk_ref[:, cols] * xs) * (z * jax.nn.sigmoid(z))
            ms = jnp.mean(y * y, axis=-1, keepdims=True)
            ssm = y * lax.rsqrt(ms + RMS_EPS) * nw_ref[:, cols]
            gate = jax.nn.sigmoid(gb_ref[:, cols] + bb_ref[:, cols])
            o_ref[:, cols] = (att_ref[:, cols] + gate * ssm).astype(o_ref.dtype)
        else:
            o_ref[:, cols] = y


SSD_GROUPS_PER_STEP = 8


def _ssd(xa, dtt, cumt, batch, seq, reverse, finish_args=None):
    m = batch * seq
    nc = seq // CHUNK
    gps = SSD_GROUPS_PER_STEP
    gw = gps * GROUP_WIDTH
    nw = gps * SSM_STATE
    assert SSM_INNER % nw == 0 and (SSM_GROUPS * SSM_STATE) % nw == 0
    assert REST_Z % gw == 0 and (REST_GATE + D_MODEL) % gw == 0 and D_MODEL % gw == 0
    b_col = SSM_INNER // nw
    c_col = b_col + SSM_GROUPS // gps
    blocks_per_chunk = 2 * SSM_GROUPS // gps
    direction = 1 if reverse else 0

    def row(b, c):
        return b * nc + ((nc - 1 - c) if reverse else c)

    def blk(width, col0):
        return pl.BlockSpec((CHUNK, width), lambda b, g, c: (row(b, c), col0 + g))

    def small():
        return pl.BlockSpec((gps * HEADS_PER_GROUP, CHUNK),
                            lambda b, g, c: (row(b, c) * blocks_per_chunk + direction * (SSM_GROUPS // gps) + g, 0))

    def vec(col0):
        return pl.BlockSpec((1, gw), lambda b, g, c: (0, col0 + g))

    in_specs = [blk(gw, 0), blk(nw, b_col), blk(nw, c_col), small(), small()]
    args = [xa, xa, xa, dtt, cumt]
    finish = finish_args is not None
    if finish:
        y_bwd, rest, gated_attn, d_skip_wide, norm_w, gate_bias = finish_args
        in_specs += [blk(gw, 0), blk(gw, REST_Z // gw), blk(gw, (REST_GATE + D_MODEL) // gw), blk(gw, 0),
                     vec(0), vec(0), vec(D_MODEL // gw)]
        args += [y_bwd, rest, rest, gated_attn, d_skip_wide, norm_w, gate_bias]
    return pl.pallas_call(
        functools.partial(_ssd_kernel, reverse=reverse, finish=finish),
        grid=(batch, SSM_GROUPS // gps, nc),
        in_specs=in_specs,
        out_specs=blk(gw, 0),
        out_shape=jax.ShapeDtypeStruct((m, SSM_INNER), BF16 if finish else F32),
        scratch_shapes=[pltpu.VMEM((gps, SSM_STATE, GROUP_WIDTH), F32)],
        compiler_params=_params(("parallel", "parallel", "arbitrary")),
        name="ssd_fwd_merge" if finish else "ssd_bwd",
    )(*args)


def _proj_ln_kernel(h_ref, w_ref, x_ref, g_ref, b_ref, o_ref):
    j = pl.program_id(1)

    @pl.when(j == 0)
    def _():
        o_ref[...] = jnp.zeros_like(o_ref)

    o_ref[...] += jnp.dot(h_ref[...], w_ref[...], preferred_element_type=F32)

    @pl.when(j == pl.num_programs(1) - 1)
    def _():
        _residual_ln_inplace(x_ref, o_ref, g_ref, b_ref, 1.0)


def _proj_ln(h, w, x, ln_g, ln_b, tm=ROW_TILE, tk=1024):
    m, k = h.shape
    tm = min(tm, m)
    d = w.shape[1]
    return pl.pallas_call(
        _proj_ln_kernel,
        grid=(m // tm, k // tk),
        in_specs=[
            pl.BlockSpec((tm, tk), lambda i, j: (i, j)),
            pl.BlockSpec((tk, d), lambda i, j: (j, 0)),
            pl.BlockSpec((tm, d), lambda i, j: (i, 0), pipeline_mode=pl.Buffered(1)),
            pl.BlockSpec((1, d), lambda i, j: (0, 0)),
            pl.BlockSpec((1, d), lambda i, j: (0, 0)),
        ],
        out_specs=pl.BlockSpec((tm, d), lambda i, j: (i, 0), pipeline_mode=pl.Buffered(1)),
        out_shape=jax.ShapeDtypeStruct((m, d), F32),
        compiler_params=_params(("parallel", "arbitrary")),
        name="proj_ln",
    )(h, w, x, ln_g, ln_b)


def _prepare(ffn1_w_gate, ffn1_w_up, ffn1_w_down, ln1_g, ln1_b, w_in, conv_w, conv_b, dt_bias, a_log,
             d_skip, ssm_norm_w, sink, gate_bias, w_out, ln2_g, ln2_b,
             ffn2_w_gate, ffn2_w_up, ffn2_w_down, ln3_g, ln3_b):
    row = lambda v: v.reshape(1, -1).astype(F32)
    dt0 = QKV_WIDTH + SSM_INNER + XBC_WIDTH
    dt1 = dt0 + 2 * SSM_HEADS
    z1 = QKV_WIDTH + SSM_INNER
    w_rest = jnp.concatenate(
        [w_in[:, QKV_WIDTH:z1], w_in[:, dt1:], w_in[:, z1:dt0], w_in[:, dt0:dt1],
         jnp.zeros((D_MODEL, REST_WIDTH - (REST_DT + 2 * SSM_HEADS)), w_in.dtype)], axis=1)
    slopes = 2.0 ** (-8.0 * jnp.arange(1, N_HEADS + 1, dtype=F32) / N_HEADS)
    return dict(
        ffn1=(ffn1_w_gate.astype(BF16), ffn1_w_up.astype(BF16), ffn1_w_down.astype(BF16), row(ln1_g), row(ln1_b)),
        w_qkv=w_in[:, :QKV_WIDTH].astype(BF16),
        w_rest=w_rest.astype(BF16),
        conv_w=conv_w.astype(F32), conv_b=row(conv_b),
        dt_bias=row(dt_bias), a_log=row(a_log),
        d_skip_wide=row(jnp.repeat(d_skip.astype(F32), SSM_HEAD_DIM)),
        norm_w=row(ssm_norm_w),
        head_params=jnp.stack([slopes, sink.astype(F32)]),
        gate_bias=row(gate_bias),
        w_out=w_out.astype(BF16), ln2=(row(ln2_g), row(ln2_b)),
        ffn2=(ffn2_w_gate.astype(BF16), ffn2_w_up.astype(BF16), ffn2_w_down.astype(BF16), row(ln3_g), row(ln3_b)),
    )


def _layer(x, p):
    batch, seq, d = x.shape
    x2d = x.reshape(batch * seq, d)
    x1, x1b = _ffn_ln(x2d, *p["ffn1"], emit_bf16=True)
    qkv = _matmul(x1b, p["w_qkv"], BF16, name="proj_qkv")
    rest = _matmul(x1b, p["w_rest"], F32, name="proj_rest")
    gated_attn = _attention(qkv, rest, p["head_params"], p["gate_bias"], batch, seq)
    xa = _conv_silu(rest, p["conv_w"], p["conv_b"], batch, seq)
    dtt, cumt = _dt_prep(rest, p["dt_bias"], p["a_log"])
    y_bwd = _ssd(xa, dtt, cumt, batch, seq, reverse=True)
    h = _ssd(xa, dtt, cumt, batch, seq, reverse=False,
             finish_args=(y_bwd, rest, gated_attn, p["d_skip_wide"], p["norm_w"], p["gate_bias"]))
    x2 = _proj_ln(h, p["w_out"], x1, *p["ln2"])
    y = _ffn_ln(x2, *p["ffn2"], emit_bf16=False)
    return y.reshape(batch, seq, d)


def kernel(x_prompt, x_sample, ffn1_w_gate, ffn1_w_up, ffn1_w_down, ln1_g, ln1_b, w_in, conv_w, conv_b, dt_bias, a_log, d_skip, ssm_norm_w, sink, gate_bias, w_out, ln2_g, ln2_b, ffn2_w_gate, ffn2_w_up, ffn2_w_down, ln3_g, ln3_b):
    weights = (ffn1_w_gate, ffn1_w_up, ffn1_w_down, ln1_g, ln1_b, w_in, conv_w, conv_b, dt_bias, a_log,
               d_skip, ssm_norm_w, sink, gate_bias, w_out, ln2_g, ln2_b,
               ffn2_w_gate, ffn2_w_up, ffn2_w_down, ln3_g, ln3_b)
    p = _prepare(*[w[0] for w in weights])
    return (_layer(x_prompt, p), _layer(x_sample, p))
```
